```python
import math
import jax, jax.numpy as jnp
from jax import lax
import numpy as np

D_MODEL = 1024
BATCH = 4
SEQ = 8192
DEPTH = 1

CHUNK = 64
Q_BLOCK = 128
ROPE_THETA = 10000.0
EPS = 1e-6

DA_HEADS = 4
DA_HEAD_DIM = 64
DA_V_DIM = 2 * DA_HEAD_DIM
DA_WIDTH = DA_HEADS * DA_V_DIM

SG_GROUPS = 4
SG_BLOCK = 128
SG_GROUP_DIM = 128
SG_WIDTH = SG_GROUPS * SG_GROUP_DIM

IN_WIDTH = 3 * DA_WIDTH + 2 * SG_WIDTH

PEER_HEADS = 8
PEER_NKEYS = 128
PEER_N_EXPERTS = PEER_NKEYS * PEER_NKEYS
PEER_KEY_DIM = 256
PEER_HALF = PEER_KEY_DIM // 2
PEER_TOPK = 16
PEER_TOKEN_BLOCK = 128

kernel_name = "chunk_causal_diffattn_sgu_peer_block"


def rmsnorm(x, g):
    xf = x.astype(jnp.float32)
    y = xf * lax.rsqrt(jnp.mean(xf * xf, axis=-1, keepdims=True) + EPS)
    return (y * g.astype(jnp.float32)).astype(x.dtype)


def layernorm(x, g, b):
    xf = x.astype(jnp.float32)
    mu = jnp.mean(xf, axis=-1, keepdims=True)
    xc = xf - mu
    y = xc * lax.rsqrt(jnp.mean(xc * xc, axis=-1, keepdims=True) + EPS)
    return (y * g.astype(jnp.float32) + b.astype(jnp.float32)).astype(x.dtype)


def rope_tables(seq, dim):
    inv = 1.0 / (ROPE_THETA ** (jnp.arange(0, dim, 2, dtype=jnp.float32) / dim))
    ang = jnp.arange(seq, dtype=jnp.float32)[:, None] * inv[None, :]
    return jnp.cos(ang), jnp.sin(ang)


def apply_rope(x, cos, sin):
    d2 = x.shape[-1] // 2
    x1, x2 = x[..., :d2], x[..., d2:]
    c = cos[None, :, None, None, :].astype(x.dtype)
    s = sin[None, :, None, None, :].astype(x.dtype)
    return jnp.concatenate([x1 * c - x2 * s, x1 * s + x2 * c], axis=-1)


def diff_attention(q, k, v, lam, subln_g, lambda_init):
    B, S, H, _, d = q.shape
    nblk = S // Q_BLOCK
    scale = d ** -0.5
    qb = q.reshape(B, nblk, Q_BLOCK, H, 2, d).transpose(1, 0, 2, 3, 4, 5)
    key_chunk = jnp.arange(S) // CHUNK

    def one_block(args):
        i, qi = args
        s = jnp.einsum('bqhcd,bkhcd->bhcqk', qi, k).astype(jnp.float32) * scale
        q_chunk = (i * Q_BLOCK + jnp.arange(Q_BLOCK)) // CHUNK
        mask = key_chunk[None, :] <= q_chunk[:, None]
        s = jnp.where(mask, s, -jnp.inf)
        p = jax.nn.softmax(s, axis=-1)
        a = p[:, :, 0] - lam * p[:, :, 1]
        return jnp.einsum('bhqk,bkhe->bqhe', a.astype(v.dtype), v)

    o = lax.map(one_block, (jnp.arange(nblk), qb))
    o = o.transpose(1, 0, 2, 3, 4).reshape(B, S, H, 2 * d)
    o = rmsnorm(o, subln_g) * (1.0 - lambda_init)
    return o.reshape(B, S, H * 2 * d)


def spatial_gating(u, v, ln_g, ln_b, w_s, b_s):
    B, S, _ = v.shape
    v = layernorm(v, ln_g, ln_b)
    n = S // SG_BLOCK
    vb = v.reshape(B, n, SG_BLOCK, SG_GROUPS, SG_GROUP_DIM)
    pc = jnp.arange(SG_BLOCK) // CHUNK
    mask = (pc[None, :] <= pc[:, None]).astype(w_s.dtype)
    s = jnp.einsum('gij,bnjgc->bnigc', w_s * mask[None], vb) + b_s.T[None, None, :, :, None]
    return u * s.reshape(B, S, SG_WIDTH)


def peer(h, w_q, sub_keys, u_tab, v_tab):
    B, S, D = h.shape
    T = B * S
    K = PEER_TOPK
    hf = h.reshape(T, D)
    q = (hf @ w_q).reshape(T, PEER_HEADS, 2, PEER_HALF)
    sc = jnp.einsum('thpc,hpkc->thpk', q, sub_keys).astype(jnp.float32)
    s1, i1 = lax.top_k(sc[:, :, 0], K)
    s2, i2 = lax.top_k(sc[:, :, 1], K)
    cand = (s1[..., :, None] + s2[..., None, :]).reshape(T, PEER_HEADS, K * K)
    cidx = (i1[..., :, None] * PEER_NKEYS + i2[..., None, :]).reshape(T, PEER_HEADS, K * K)
    top_s, pos = lax.top_k(cand, K)
    idx = jnp.take_along_axis(cidx, pos, axis=-1)
    g = jax.nn.softmax(top_s, axis=-1).astype(h.dtype)

    tb = PEER_TOKEN_BLOCK
    nb = T // tb
    HK = PEER_HEADS * K

    def one_block(args):
        xb, ib, gb = args
        a = jax.nn.gelu(jnp.einsum('td,ted->te', xb, u_tab[ib]), approximate=False)
        return jnp.einsum('te,ted->td', gb * a, v_tab[ib])

    out = lax.map(one_block, (hf.reshape(nb, tb, D), idx.reshape(nb, tb, HK), g.reshape(nb, tb, HK)))
    return out.reshape(B, S, D)


def setup_inputs(seed: int = 0) -> dict:
    key = jax.random.key(seed)
    ks = jax.random.split(key, 24)
    D = D_MODEL
    nrm = lambda k, shape, std: jax.random.normal(k, shape, jnp.float32) * std
    return {
        "x": nrm(ks[0], (BATCH, SEQ, D), 1.0),
        "norm1_g": 1.0 + nrm(ks[1], (DEPTH, D), 0.02),
        "w_in": nrm(ks[2], (DEPTH, D, IN_WIDTH), D ** -0.5),
        "lambda_q1": nrm(ks[3], (DEPTH, DA_HEAD_DIM), 0.1),
        "lambda_k1": nrm(ks[4], (DEPTH, DA_HEAD_DIM), 0.1),
        "lambda_q2": nrm(ks[5], (DEPTH, DA_HEAD_DIM), 0.1),
        "lambda_k2": nrm(ks[6], (DEPTH, DA_HEAD_DIM), 0.1),
        "subln_g": 1.0 + nrm(ks[7], (DEPTH, DA_V_DIM), 0.02),
        "sgu_ln_g": 1.0 + nrm(ks[8], (DEPTH, SG_WIDTH), 0.02),
        "sgu_ln_b": nrm(ks[9], (DEPTH, SG_WIDTH), 0.02),
        "w_spatial": nrm(ks[10], (DEPTH, SG_GROUPS, SG_BLOCK, SG_BLOCK), SG_BLOCK ** -0.5),
        "b_spatial": 1.0 + nrm(ks[11], (DEPTH, SG_GROUPS, SG_BLOCK), 0.02),
        "w_branch_a": nrm(ks[12], (DEPTH, DA_WIDTH, D), DA_WIDTH ** -0.5),
        "w_branch_b": nrm(ks[13], (DEPTH, SG_WIDTH, D), SG_WIDTH ** -0.5),
        "w_gate": nrm(ks[14], (DEPTH, D, 2 * D), D ** -0.5),
        "b_gate": nrm(ks[15], (DEPTH, 2 * D), 0.02),
        "w_out": nrm(ks[16], (DEPTH, D, D), D ** -0.5),
        "norm2_g": 1.0 + nrm(ks[17], (DEPTH, D), 0.02),
        "peer_wq": nrm(ks[18], (DEPTH, D, PEER_HEADS * PEER_KEY_DIM), D ** -0.5),
        "peer_subkeys": nrm(ks[19], (DEPTH, PEER_HEADS, 2, PEER_NKEYS, PEER_HALF), PEER_HALF ** -0.5),
        "peer_u": nrm(ks[20], (DEPTH, PEER_N_EXPERTS, D), D ** -0.5),
        "peer_v": nrm(ks[21], (DEPTH, PEER_N_EXPERTS, D), 0.5),
        "norm_f_g": 1.0 + nrm(ks[22], (D,), 0.02),
    }


def reference(x, norm1_g, w_in, lambda_q1, lambda_k1, lambda_q2, lambda_k2, subln_g,
              sgu_ln_g, sgu_ln_b, w_spatial, b_spatial, w_branch_a, w_branch_b,
              w_gate, b_gate, w_out, norm2_g, peer_wq, peer_subkeys, peer_u, peer_v,
              norm_f_g):
    B, S, D = x.shape
    cos, sin = rope_tables(S, DA_HEAD_DIM)
    for l in range(DEPTH):
        lambda_init = 0.8 - 0.6 * math.exp(-0.3 * l)
        h = rmsnorm(x, norm1_g[l])
        proj = h @ w_in[l]
        qa, ka, va, su, sv = jnp.split(proj, np.cumsum([DA_WIDTH, DA_WIDTH, DA_WIDTH, SG_WIDTH]).tolist(), axis=-1)
        qa = apply_rope(qa.reshape(B, S, DA_HEADS, 2, DA_HEAD_DIM), cos, sin)
        ka = apply_rope(ka.reshape(B, S, DA_HEADS, 2, DA_HEAD_DIM), cos, sin)
        va = va.reshape(B, S, DA_HEADS, DA_V_DIM)
        lam = (jnp.exp(jnp.sum(lambda_q1[l].astype(jnp.float32) * lambda_k1[l].astype(jnp.float32)))
               - jnp.exp(jnp.sum(lambda_q2[l].astype(jnp.float32) * lambda_k2[l].astype(jnp.float32)))
               + lambda_init)
        y_a = diff_attention(qa, ka, va, lam, subln_g[l], lambda_init)
        y_b = spatial_gating(jax.nn.gelu(su, approximate=False), jax.nn.gelu(sv, approximate=False),
                             sgu_ln_g[l], sgu_ln_b[l], w_spatial[l], b_spatial[l])
        gates = jax.nn.sigmoid(h @ w_gate[l] + b_gate[l]).reshape(B, S, 2, D)
        merged = gates[:, :, 0] * (y_a @ w_branch_a[l]) + gates[:, :, 1] * (y_b @ w_branch_b[l])
        x = x + merged @ w_out[l]
        x = x + peer(rmsnorm(x, norm2_g[l]), peer_wq[l], peer_subkeys[l], peer_u[l], peer_v[l])
    return rmsnorm(x, norm_f_g)
```

```python
import functools
import math

import jax
import jax.numpy as jnp
from jax import lax
from jax.experimental import pallas as pl
from jax.experimental.pallas import tpu as pltpu

F32 = jnp.float32
BF16 = jnp.bfloat16
I32 = jnp.int32

D_MODEL = 1024
CHUNK = 64
ROPE_THETA = 10000.0
EPS = 1e-6

DA_HEADS = 4
DA_HEAD_DIM = 64
DA_V_DIM = 2 * DA_HEAD_DIM
DA_WIDTH = DA_HEADS * DA_V_DIM

SG_GROUPS = 4
SG_BLOCK = 128
SG_GROUP_DIM = 128
SG_WIDTH = SG_GROUPS * SG_GROUP_DIM

PEER_HEADS = 8
PEER_NKEYS = 128
PEER_HALF = 128
PEER_TOPK = 16
PEER_SLOTS = PEER_HEADS * PEER_TOPK
PEER_N_EXPERTS = PEER_NKEYS * PEER_NKEYS
PEER_PAIR_ROWS = PEER_N_EXPERTS // 2

LANES = 128
SUBLANES = 8
PACK_ROWS = 16
D_CHUNKS = D_MODEL // LANES
GATHER_STRIDE = PEER_SLOTS + 1

VMEM_LIMIT = 48 * 1024 * 1024
VMEM_LIMIT_TABLE = 56 * 1024 * 1024

LAMBDA_INIT = 0.8 - 0.6 * math.exp(-0.3 * 0)


def _gelu(x):
    return 0.5 * x * (1.0 + lax.erf(x * (0.5 ** 0.5)))


def _rms(x, g):
    return x * lax.rsqrt(jnp.mean(x * x, axis=-1, keepdims=True) + EPS) * g


def _dot(a, b):
    return jnp.dot(a, b, preferred_element_type=F32)


def _dot_t(a, b):
    return lax.dot_general(a, b, (((1,), (1,)), ((), ())), preferred_element_type=F32)


def _proj_kernel(x_ref, g1_ref, w_ref, bg_ref, cos_ref, sin_ref, lng_ref, lnb_ref,
                 q_ref, k_ref, v_ref, u_ref, sv_ref, gate_ref):
    tm = x_ref.shape[0]
    hb = _rms(x_ref[...], g1_ref[...]).astype(BF16)
    cos = cos_ref[...]
    sin = sin_ref[...]
    lane = lax.broadcasted_iota(I32, (tm, LANES), 1)
    first_half = (lane % DA_HEAD_DIM) < (DA_HEAD_DIM // 2)

    def rope(t):
        outs = []
        for b in range(DA_WIDTH // LANES):
            blk = t[:, b * LANES:(b + 1) * LANES]
            swapped = jnp.where(first_half, pltpu.roll(blk, LANES - DA_HEAD_DIM // 2, 1),
                                pltpu.roll(blk, DA_HEAD_DIM // 2, 1))
            outs.append(blk * cos + swapped * sin)
        return jnp.concatenate(outs, axis=1)

    o = 0
    q = _dot(hb, w_ref[:, o:o + DA_WIDTH]); o += DA_WIDTH
    q_ref[...] = (rope(q) * (DA_HEAD_DIM ** -0.5)).astype(BF16)
    k = _dot(hb, w_ref[:, o:o + DA_WIDTH]); o += DA_WIDTH
    k_ref[...] = rope(k).astype(BF16)
    v_ref[...] = _dot(hb, w_ref[:, o:o + DA_WIDTH]).astype(BF16); o += DA_WIDTH
    u_ref[...] = _gelu(_dot(hb, w_ref[:, o:o + SG_WIDTH])).astype(BF16); o += SG_WIDTH
    sv = _gelu(_dot(hb, w_ref[:, o:o + SG_WIDTH])); o += SG_WIDTH
    mu = jnp.mean(sv, axis=-1, keepdims=True)
    svc = sv - mu
    sv_ref[...] = (svc * lax.rsqrt(jnp.mean(svc * svc, axis=-1, keepdims=True) + EPS) * lng_ref[...]
                   + lnb_ref[...]).astype(BF16)
    gate_ref[...] = jax.nn.sigmoid(_dot(hb, w_ref[:, o:o + 2 * D_MODEL]) + bg_ref[...]).astype(BF16)


def _proj_call(x2, g1, w_cat, b_gate, cos_t, sin_t, ln_g, ln_b, seq, tm):
    t = x2.shape[0]
    nseq = seq // tm
    row = lambda i: (i, 0)
    fixed = lambda i: (0, 0)
    pos = lambda i: (i % nseq, 0)
    wcols = w_cat.shape[1]
    out_shapes = [jax.ShapeDtypeStruct((t, DA_WIDTH), BF16)] * 3 + \
                 [jax.ShapeDtypeStruct((t, SG_WIDTH), BF16)] * 2 + \
                 [jax.ShapeDtypeStruct((t, 2 * D_MODEL), BF16)]
    return pl.pallas_call(
        _proj_kernel,
        grid=(t // tm,),
        in_specs=[
            pl.BlockSpec((tm, D_MODEL), row),
            pl.BlockSpec((1, D_MODEL), fixed),
            pl.BlockSpec((D_MODEL, wcols), fixed),
            pl.BlockSpec((1, 2 * D_MODEL), fixed),
            pl.BlockSpec((tm, LANES), pos),
            pl.BlockSpec((tm, LANES), pos),
            pl.BlockSpec((1, SG_WIDTH), fixed),
            pl.BlockSpec((1, SG_WIDTH), fixed),
        ],
        out_specs=[pl.BlockSpec((tm, DA_WIDTH), row)] * 3 + [pl.BlockSpec((tm, SG_WIDTH), row)] * 2 +
                  [pl.BlockSpec((tm, 2 * D_MODEL), row)],
        out_shape=out_shapes,
        compiler_params=pltpu.CompilerParams(dimension_semantics=("arbitrary",), vmem_limit_bytes=VMEM_LIMIT),
        name="proj",
    )(x2, g1, w_cat, b_gate, cos_t, sin_t, ln_g, ln_b)


def _attn_kernel(lam_ref, q_ref, k_ref, v_ref, sg_ref, o_ref, acc_ref, m_ref, l_ref, *, tq, tk):
    i = pl.program_id(2)
    q = q_ref[0]
    lane = lax.broadcasted_iota(I32, (tq, DA_V_DIM), 1)
    zero = jnp.zeros_like(q)
    qs = (jnp.where(lane < DA_HEAD_DIM, q, zero), jnp.where(lane >= DA_HEAD_DIM, q, zero))

    m_ref[...] = jnp.full(m_ref.shape, -1e30, F32)
    l_ref[...] = jnp.zeros(l_ref.shape, F32)
    acc_ref[...] = jnp.zeros(acc_ref.shape, F32)

    def step(j, masked):
        start = pl.multiple_of(j * tk, tk)
        kb = k_ref[0, pl.ds(start, tk), :]
        vb = v_ref[0, pl.ds(start, tk), :]
        if masked:
            rowc = (i * tq + lax.broadcasted_iota(I32, (tq, tk), 0)) // CHUNK
            colc = (j * tk + lax.broadcasted_iota(I32, (tq, tk), 1)) // CHUNK
            visible = colc <= rowc
        for c in range(2):
            s = _dot_t(qs[c], kb)
            if masked:
                s = jnp.where(visible, s, -jnp.inf)
            m_old = m_ref[c]
            m_new = jnp.maximum(m_old, jnp.max(s, axis=-1, keepdims=True))
            alpha = jnp.exp(m_old - m_new)
            p = jnp.exp(s - m_new)
            l_ref[c] = alpha * l_ref[c] + jnp.sum(p, axis=-1, keepdims=True)
            acc_ref[c] = alpha * acc_ref[c] + _dot(p.astype(BF16), vb)
            m_ref[c] = m_new

    nfull = i * (tq // tk)

    def body(j, carry):
        step(j, False)
        return carry

    lax.fori_loop(0, nfull, body, 0)
    for jj in range(tq // tk):
        step(nfull + jj, True)

    lam = (jnp.exp(jnp.sum(lam_ref[0:1, :] * lam_ref[1:2, :], axis=-1, keepdims=True))
           - jnp.exp(jnp.sum(lam_ref[2:3, :] * lam_ref[3:4, :], axis=-1, keepdims=True)) + LAMBDA_INIT)
    o = acc_ref[0] / l_ref[0] - lam * (acc_ref[1] / l_ref[1])
    o_ref[0] = (_rms(o, sg_ref[...]) * (1.0 - LAMBDA_INIT)).astype(BF16)


def _attn_call(lam4, q3, k3, v3, subln_g, tq, tk):
    b, s, _ = q3.shape
    kern = functools.partial(_attn_kernel, tq=tq, tk=tk)
    return pl.pallas_call(
        kern,
        grid=(b, DA_HEADS, s // tq),
        in_specs=[
            pl.BlockSpec((4, DA_HEAD_DIM), lambda bi, h, i: (0, 0)),
            pl.BlockSpec((1, tq, DA_V_DIM), lambda bi, h, i: (bi, i, h)),
            pl.BlockSpec((1, s, DA_V_DIM), lambda bi, h, i: (bi, 0, h)),
            pl.BlockSpec((1, s, DA_V_DIM), lambda bi, h, i: (bi, 0, h)),
            pl.BlockSpec((1, DA_V_DIM), lambda bi, h, i: (0, 0)),
        ],
        out_specs=pl.BlockSpec((1, tq, DA_V_DIM), lambda bi, h, i: (bi, i, h)),
        out_shape=jax.ShapeDtypeStruct((b, s, DA_WIDTH), BF16),
        scratch_shapes=[
            pltpu.VMEM((2, tq, DA_V_DIM), F32),
            pltpu.VMEM((2, tq, 1), F32),
            pltpu.VMEM((2, tq, 1), F32),
        ],
        compiler_params=pltpu.CompilerParams(dimension_semantics=("arbitrary",) * 3, vmem_limit_bytes=VMEM_LIMIT),
        name="attn",
    )(lam4, q3, k3, v3, subln_g)


def _mix_kernel(x_ref, u_ref, sv_ref, gate_ref, ya_ref, ws_ref, bs_ref, wa_ref, wb_ref, wo_ref, g2_ref,
                wq_ref, sk_ref, x1_ref, h2_ref, sc_ref):
    tm = x_ref.shape[0]
    pc = lax.broadcasted_iota(I32, (SG_BLOCK, SG_BLOCK), 0) // CHUNK
    pr = lax.broadcasted_iota(I32, (SG_BLOCK, SG_BLOCK), 1) // CHUNK
    causal = pr <= pc
    rows = []
    for n in range(tm // SG_BLOCK):
        cols = []
        for g in range(SG_GROUPS):
            w = jnp.where(causal, ws_ref[g], 0.0).astype(BF16)
            vb = sv_ref[n * SG_BLOCK:(n + 1) * SG_BLOCK, g * SG_GROUP_DIM:(g + 1) * SG_GROUP_DIM]
            cols.append(_dot(w, vb))
        rows.append(jnp.concatenate(cols, axis=1) + bs_ref[...])
    s = jnp.concatenate(rows, axis=0)
    yb = (u_ref[...].astype(F32) * s).astype(BF16)
    merged = (gate_ref[:, :D_MODEL].astype(F32) * _dot(ya_ref[...], wa_ref[...])
              + gate_ref[:, D_MODEL:].astype(F32) * _dot(yb, wb_ref[...]))
    x1 = x_ref[...] + _dot(merged.astype(BF16), wo_ref[...])
    x1_ref[...] = x1
    h2 = _rms(x1, g2_ref[...]).astype(BF16)
    h2_ref[...] = h2
    qp = _dot(h2, wq_ref[...]).astype(BF16)
    for hp in range(2 * PEER_HEADS):
        sc_ref[hp] = _dot_t(sk_ref[hp], qp[:, hp * PEER_HALF:(hp + 1) * PEER_HALF])


def _mix_call(x2, u_act, sv_ln, gates, ya, w_s, b_s_full, w_a, w_b, w_o, g2, w_q, sk, tm):
    t = x2.shape[0]
    row = lambda i: (i, 0)
    fixed2 = lambda i: (0, 0)
    fixed3 = lambda i: (0, 0, 0)
    return pl.pallas_call(
        _mix_kernel,
        grid=(t // tm,),
        in_specs=[
            pl.BlockSpec((tm, D_MODEL), row),
            pl.BlockSpec((tm, SG_WIDTH), row),
            pl.BlockSpec((tm, SG_WIDTH), row),
            pl.BlockSpec((tm, 2 * D_MODEL), row),
            pl.BlockSpec((tm, DA_WIDTH), row),
            pl.BlockSpec((SG_GROUPS, SG_BLOCK, SG_BLOCK), fixed3),
            pl.BlockSpec((SG_BLOCK, SG_WIDTH), fixed2),
            pl.BlockSpec((DA_WIDTH, D_MODEL), fixed2),
            pl.BlockSpec((SG_WIDTH, D_MODEL), fixed2),
            pl.BlockSpec((D_MODEL, D_MODEL), fixed2),
            pl.BlockSpec((1, D_MODEL), fixed2),
            pl.BlockSpec((D_MODEL, 2 * PEER_HEADS * PEER_HALF), fixed2),
            pl.BlockSpec((2 * PEER_HEADS, PEER_NKEYS, PEER_HALF), fixed3),
        ],
        out_specs=[
            pl.BlockSpec((tm, D_MODEL), row),
            pl.BlockSpec((tm, D_MODEL), row),
            pl.BlockSpec((2 * PEER_HEADS, PEER_NKEYS, tm), lambda i: (0, 0, i)),
        ],
        out_shape=[
            jax.ShapeDtypeStruct((t, D_MODEL), F32),
            jax.ShapeDtypeStruct((t, D_MODEL), BF16),
            jax.ShapeDtypeStruct((2 * PEER_HEADS, PEER_NKEYS, t), F32),
        ],
        compiler_params=pltpu.CompilerParams(dimension_semantics=("arbitrary",), vmem_limit_bytes=VMEM_LIMIT),
        name="mix",
    )(x2, u_act, sv_ln, gates, ya, w_s, b_s_full, w_a, w_b, w_o, g2, w_q, sk)


def _top16(vals, n):
    tt = vals.shape[1]
    pos = lax.broadcasted_iota(I32, (n, tt), 0)
    cur = vals
    out_v, out_i = [], []
    for _ in range(PEER_TOPK):
        m = jnp.max(cur, axis=0, keepdims=True)
        idx = jnp.min(jnp.where(cur == m, pos, n), axis=0, keepdims=True)
        out_v.append(m)
        out_i.append(idx)
        cur = jnp.where(pos == idx, -jnp.inf, cur)
    return out_v, out_i


def _topk_kernel(sc_ref, row_ref, g0_ref, g1_ref):
    tt = sc_ref.shape[2]
    k = PEER_TOPK
    for h in range(PEER_HEADS):
        s1, i1 = _top16(sc_ref[2 * h], PEER_NKEYS)
        s2, i2 = _top16(sc_ref[2 * h + 1], PEER_NKEYS)
        s2m = jnp.concatenate(s2, axis=0)
        i2m = jnp.concatenate(i2, axis=0)
        cand = jnp.concatenate([s1[a] + s2m for a in range(k)], axis=0)
        cidx = jnp.concatenate([i1[a] * PEER_NKEYS + i2m for a in range(k)], axis=0)
        top_s, pos = _top16(cand, k * k)
        cpos = lax.broadcasted_iota(I32, (k * k, tt), 0)
        e = [jnp.sum(jnp.where(cpos == pos[r], cidx, 0), axis=0, keepdims=True) for r in range(k)]
        w = [jnp.exp(top_s[r] - top_s[0]) for r in range(k)]
        den = w[0]
        for r in range(1, k):
            den = den + w[r]
        em = jnp.concatenate(e, axis=0)
        gm = jnp.concatenate(w, axis=0) / den
        upper = em >= PEER_PAIR_ROWS
        row_ref[h * k:(h + 1) * k, :] = jnp.where(upper, em - PEER_PAIR_ROWS, em) * D_CHUNKS
        g0_ref[h * k:(h + 1) * k, :] = jnp.where(upper, 0.0, gm)
        g1_ref[h * k:(h + 1) * k, :] = jnp.where(upper, gm, 0.0)


def _topk_call(sc, tt):
    t = sc.shape[2]
    blk = pl.BlockSpec((PEER_SLOTS, tt), lambda i: (0, i))
    return pl.pallas_call(
        _topk_kernel,
        grid=(t // tt,),
        in_specs=[pl.BlockSpec((2 * PEER_HEADS, PEER_NKEYS, tt), lambda i: (0, 0, i))],
        out_specs=[blk, blk, blk],
        out_shape=[jax.ShapeDtypeStruct((PEER_SLOTS, t), I32),
                   jax.ShapeDtypeStruct((PEER_SLOTS, t), F32),
                   jax.ShapeDtypeStruct((PEER_SLOTS, t), F32)],
        compiler_params=pltpu.CompilerParams(dimension_semantics=("arbitrary",), vmem_limit_bytes=VMEM_LIMIT),
        name="topk",
    )(sc)


def _gather_rows(row_ref, tab_ref, tile_ref, t):
    for j in range(PEER_SLOTS):
        r = pl.multiple_of(row_ref[j, t], D_CHUNKS)
        tile_ref[pl.ds(j, D_CHUNKS, stride=GATHER_STRIDE), :] = tab_ref[pl.ds(r, D_CHUNKS), :]


def _chunk_operand(tile_ref, c):
    return pltpu.bitcast(tile_ref[pl.ds(c * GATHER_STRIDE, PEER_SLOTS), :], BF16)


def _row_select(t):
    sub = lax.broadcasted_iota(I32, (PACK_ROWS, LANES), 0)
    return sub == (t % PACK_ROWS)


def _peer_u_kernel(row_ref, h2_ref, g0_ref, g1_ref, tab_ref, coef_ref, tile_ref, acc_ref):
    tt = h2_ref.shape[0]

    def body(t, carry):
        _gather_rows(row_ref, tab_ref, tile_ref, t)
        base = pl.multiple_of((t // PACK_ROWS) * PACK_ROWS, PACK_ROWS)
        xg = h2_ref[pl.ds(base, PACK_ROWS), :]
        res = _dot_t(xg[:, 0:LANES], _chunk_operand(tile_ref, 0))
        for c in range(1, D_CHUNKS):
            res = res + _dot_t(xg[:, c * LANES:(c + 1) * LANES], _chunk_operand(tile_ref, c))
        sel = jnp.concatenate([_row_select(t)] * 2, axis=1)
        acc_ref[pl.ds(base, PACK_ROWS), :] = jnp.where(sel, res, acc_ref[pl.ds(base, PACK_ROWS), :])
        return carry

    lax.fori_loop(0, tt, body, 0)
    src = lax.broadcasted_iota(I32, (PEER_SLOTS, 2 * PEER_SLOTS), 0)
    dst = lax.broadcasted_iota(I32, (PEER_SLOTS, 2 * PEER_SLOTS), 1)
    e0 = jnp.where(dst == 2 * src, 1.0, 0.0).astype(BF16)
    e1 = jnp.where(dst == 2 * src + 1, 1.0, 0.0).astype(BF16)
    w2 = _dot(g0_ref[...].T.astype(BF16), e0) + _dot(g1_ref[...].T.astype(BF16), e1)
    coef_ref[...] = (_gelu(acc_ref[...]) * w2).astype(BF16)


def _peer_u_call(rows, h2, g0, g1, u_tab, tt):
    t = h2.shape[0]
    slot_blk = lambda i: (0, i)
    return pl.pallas_call(
        _peer_u_kernel,
        grid=(t // tt,),
        in_specs=[
            pl.BlockSpec((PEER_SLOTS, tt), slot_blk, memory_space=pltpu.SMEM),
            pl.BlockSpec((tt, D_MODEL), lambda i: (i, 0)),
            pl.BlockSpec((PEER_SLOTS, tt), slot_blk),
            pl.BlockSpec((PEER_SLOTS, tt), slot_blk),
            pl.BlockSpec(u_tab.shape, lambda i: (0, 0), pipeline_mode=pl.Buffered(1)),
        ],
        out_specs=pl.BlockSpec((tt, 2 * PEER_SLOTS), lambda i: (i, 0)),
        out_shape=jax.ShapeDtypeStruct((t, 2 * PEER_SLOTS), BF16),
        scratch_shapes=[
            pltpu.VMEM((D_CHUNKS * GATHER_STRIDE, LANES), I32),
            pltpu.VMEM((tt, 2 * PEER_SLOTS), F32),
        ],
        compiler_params=pltpu.CompilerParams(dimension_semantics=("arbitrary",),
                                             vmem_limit_bytes=VMEM_LIMIT_TABLE),
        name="peer_u",
    )(rows, h2, g0, g1, u_tab)


def _peer_v_kernel(row_ref, coef_ref, x1_ref, gf_ref, tab_ref, out_ref, tile_ref, acc_ref):
    tt = coef_ref.shape[0]

    def body(t, carry):
        _gather_rows(row_ref, tab_ref, tile_ref, t)
        base = pl.multiple_of((t // PACK_ROWS) * PACK_ROWS, PACK_ROWS)
        cg = coef_ref[pl.ds(base, PACK_ROWS), :]
        sel = _row_select(t)
        for c in range(D_CHUNKS):
            res = _dot(cg, _chunk_operand(tile_ref, c))
            old = acc_ref[pl.ds(base, PACK_ROWS), c * LANES:(c + 1) * LANES]
            acc_ref[pl.ds(base, PACK_ROWS), c * LANES:(c + 1) * LANES] = jnp.where(sel, res, old)
        return carry

    lax.fori_loop(0, tt, body, 0)
    out_ref[...] = _rms(x1_ref[...] + acc_ref[...], gf_ref[...])


def _peer_v_call(rows, coef, x1, gf, v_tab, tt):
    t = x1.shape[0]
    return pl.pallas_call(
        _peer_v_kernel,
        grid=(t // tt,),
        in_specs=[
            pl.BlockSpec((PEER_SLOTS, tt), lambda i: (0, i), memory_space=pltpu.SMEM),
            pl.BlockSpec((tt, 2 * PEER_SLOTS), lambda i: (i, 0)),
            pl.BlockSpec((tt, D_MODEL), lambda i: (i, 0)),
            pl.BlockSpec((1, D_MODEL), lambda i: (0, 0)),
            pl.BlockSpec(v_tab.shape, lambda i: (0, 0), pipeline_mode=pl.Buffered(1)),
        ],
        out_specs=pl.BlockSpec((tt, D_MODEL), lambda i: (i, 0)),
        out_shape=jax.ShapeDtypeStruct((t, D_MODEL), F32),
        scratch_shapes=[
            pltpu.VMEM((D_CHUNKS * GATHER_STRIDE, LANES), I32),
            pltpu.VMEM((tt, D_MODEL), F32),
        ],
        compiler_params=pltpu.CompilerParams(dimension_semantics=("arbitrary",),
                                             vmem_limit_bytes=VMEM_LIMIT_TABLE),
        name="peer_v",
    )(rows, coef, x1, gf, v_tab)


def _pack_pairs(tab):
    bits = lax.bitcast_convert_type(tab.astype(BF16), jnp.uint16).astype(jnp.uint32)
    word = bits[:PEER_PAIR_ROWS] | (bits[PEER_PAIR_ROWS:] << 16)
    return lax.bitcast_convert_type(word, I32).reshape(PEER_PAIR_ROWS * D_CHUNKS, LANES)


def _rope_tables(seq):
    half = DA_HEAD_DIM // 2
    inv = 1.0 / (ROPE_THETA ** (jnp.arange(0, DA_HEAD_DIM, 2, dtype=F32) / DA_HEAD_DIM))
    ang = jnp.arange(seq, dtype=F32)[:, None] * inv[None, :]
    cos, sin = jnp.cos(ang), jnp.sin(ang)
    reps = LANES // DA_HEAD_DIM
    cos_t = jnp.tile(jnp.concatenate([cos, cos], axis=1), (1, reps))
    sin_t = jnp.tile(jnp.concatenate([-sin, sin], axis=1), (1, reps))
    del half
    return cos_t, sin_t


def kernel(x, norm1_g, w_in, lambda_q1, lambda_k1, lambda_q2, lambda_k2, subln_g, sgu_ln_g, sgu_ln_b, w_spatial, b_spatial, w_branch_a, w_branch_b, w_gate, b_gate, w_out, norm2_g, peer_wq, peer_subkeys, peer_u, peer_v, norm_f_g):
    b, s, d = x.shape
    t = b * s
    l = 0
    x2 = x.reshape(t, d)
    cos_t, sin_t = _rope_tables(s)
    w_cat = jnp.concatenate([w_in[l], w_gate[l]], axis=1).astype(BF16)

    q, k, v, u_act, sv_ln, gates = _proj_call(
        x2, norm1_g[l][None], w_cat, b_gate[l][None], cos_t, sin_t, sgu_ln_g[l][None], sgu_ln_b[l][None],
        seq=s, tm=min(512, s))

    lam4 = jnp.stack([lambda_q1[l], lambda_k1[l], lambda_q2[l], lambda_k2[l]]).astype(F32)
    ya = _attn_call(lam4, q.reshape(b, s, DA_WIDTH), k.reshape(b, s, DA_WIDTH), v.reshape(b, s, DA_WIDTH),
                    subln_g[l][None], tq=min(512, s), tk=min(512, s)).reshape(t, DA_WIDTH)

    b_s_full = jnp.repeat(b_spatial[l].T, SG_GROUP_DIM, axis=1)
    sk = peer_subkeys[l].reshape(2 * PEER_HEADS, PEER_NKEYS, PEER_HALF).astype(BF16)
    x1, h2, sc = _mix_call(x2, u_act, sv_ln, gates, ya, w_spatial[l], b_s_full,
                           w_branch_a[l].astype(BF16), w_branch_b[l].astype(BF16), w_out[l].astype(BF16),
                           norm2_g[l][None], peer_wq[l].astype(BF16), sk, tm=256)

    rows, g0, g1 = _topk_call(sc, tt=256)
    coef = _peer_u_call(rows, h2, g0, g1, _pack_pairs(peer_u[l]), tt=128)
    out = _peer_v_call(rows, coef, x1, norm_f_g[None], _pack_pairs(peer_v[l]), tt=128)
    return out.reshape(b, s, d)
```

```python
import functools
import math

import jax
import jax.numpy as jnp
from jax import lax
from jax.experimental import pallas as pl
from jax.experimental.pallas import tpu as pltpu

F32 = jnp.float32
BF16 = jnp.bfloat16
I32 = jnp.int32

D_MODEL = 1024
CHUNK = 64
ROPE_THETA = 10000.0
EPS = 1e-6

DA_HEADS = 4
DA_HEAD_DIM = 64
DA_V_DIM = 2 * DA_HEAD_DIM
DA_WIDTH = DA_HEADS * DA_V_DIM

SG_GROUPS = 4
SG_BLOCK = 128
SG_GROUP_DIM = 128
SG_WIDTH = SG_GROUPS * SG_GROUP_DIM

PEER_HEADS = 8
PEER_NKEYS = 128
PEER_HALF = 128
PEER_TOPK = 16
PEER_SLOTS = PEER_HEADS * PEER_TOPK
PEER_N_EXPERTS = PEER_NKEYS * PEER_NKEYS
PEER_PAIR_ROWS = PEER_N_EXPERTS // 2

LANES = 128
SUBLANES = 8
PACK_ROWS = 16
PEER_GROUP = 8
D_CHUNKS = D_MODEL // LANES
GATHER_STRIDE = PEER_SLOTS + 1

PROJ_TM = 512
ATTN_TQ = 512
ATTN_TK = 512
MIX_TM = 256
TOPK_TT = 256
PEER_TT = 128

VMEM_LIMIT = 48 * 1024 * 1024
VMEM_LIMIT_TABLE = 56 * 1024 * 1024

LAMBDA_INIT = 0.8 - 0.6 * math.exp(-0.3 * 0)


def _gelu(x):
    return 0.5 * x * (1.0 + lax.erf(x * (0.5 ** 0.5)))


def _rms(x, g):
    return x * lax.rsqrt(jnp.mean(x * x, axis=-1, keepdims=True) + EPS) * g


def _dot(a, b):
    return jnp.dot(a, b, preferred_element_type=F32)


def _dot_t(a, b):
    return lax.dot_general(a, b, (((1,), (1,)), ((), ())), preferred_element_type=F32)


def _proj_kernel(x_ref, g1_ref, w_ref, bg_ref, cos_ref, sin_ref, lng_ref, lnb_ref,
                 q_ref, k_ref, v_ref, u_ref, sv_ref, gate_ref):
    tm = x_ref.shape[0]
    hb = _rms(x_ref[...], g1_ref[...]).astype(BF16)
    cos = cos_ref[...]
    sin = sin_ref[...]
    lane = lax.broadcasted_iota(I32, (tm, LANES), 1)
    first_half = (lane % DA_HEAD_DIM) < (DA_HEAD_DIM // 2)

    def rope(t):
        outs = []
        for b in range(DA_WIDTH // LANES):
            blk = t[:, b * LANES:(b + 1) * LANES]
            swapped = jnp.where(first_half, pltpu.roll(blk, LANES - DA_HEAD_DIM // 2, 1),
                                pltpu.roll(blk, DA_HEAD_DIM // 2, 1))
            outs.append(blk * cos + swapped * sin)
        return jnp.concatenate(outs, axis=1)

    o = 0
    q = _dot(hb, w_ref[:, o:o + DA_WIDTH]); o += DA_WIDTH
    q_ref[...] = (rope(q) * (DA_HEAD_DIM ** -0.5)).astype(BF16)
    k = _dot(hb, w_ref[:, o:o + DA_WIDTH]); o += DA_WIDTH
    k_ref[...] = rope(k).astype(BF16)
    v_ref[...] = _dot(hb, w_ref[:, o:o + DA_WIDTH]).astype(BF16); o += DA_WIDTH
    u_ref[...] = _gelu(_dot(hb, w_ref[:, o:o + SG_WIDTH])).astype(BF16); o += SG_WIDTH
    sv = _gelu(_dot(hb, w_ref[:, o:o + SG_WIDTH])); o += SG_WIDTH
    mu = jnp.mean(sv, axis=-1, keepdims=True)
    svc = sv - mu
    sv_ref[...] = (svc * lax.rsqrt(jnp.mean(svc * svc, axis=-1, keepdims=True) + EPS) * lng_ref[...]
                   + lnb_ref[...]).astype(BF16)
    gate_ref[...] = jax.nn.sigmoid(_dot(hb, w_ref[:, o:o + 2 * D_MODEL]) + bg_ref[...]).astype(BF16)


def _proj_call(x2, g1, w_cat, b_gate, cos_t, sin_t, ln_g, ln_b, seq, tm):
    t = x2.shape[0]
    nseq = seq // tm
    row = lambda i: (i, 0)
    fixed = lambda i: (0, 0)
    pos = lambda i: (i % nseq, 0)
    wcols = w_cat.shape[1]
    out_shapes = [jax.ShapeDtypeStruct((t, DA_WIDTH), BF16)] * 3 + \
                 [jax.ShapeDtypeStruct((t, SG_WIDTH), BF16)] * 2 + \
                 [jax.ShapeDtypeStruct((t, 2 * D_MODEL), BF16)]
    return pl.pallas_call(
        _proj_kernel,
        grid=(t // tm,),
        in_specs=[
            pl.BlockSpec((tm, D_MODEL), row),
            pl.BlockSpec((1, D_MODEL), fixed),
            pl.BlockSpec((D_MODEL, wcols), fixed),
            pl.BlockSpec((1, 2 * D_MODEL), fixed),
            pl.BlockSpec((tm, LANES), pos),
            pl.BlockSpec((tm, LANES), pos),
            pl.BlockSpec((1, SG_WIDTH), fixed),
            pl.BlockSpec((1, SG_WIDTH), fixed),
        ],
        out_specs=[pl.BlockSpec((tm, DA_WIDTH), row)] * 3 + [pl.BlockSpec((tm, SG_WIDTH), row)] * 2 +
                  [pl.BlockSpec((tm, 2 * D_MODEL), row)],
        out_shape=out_shapes,
        compiler_params=pltpu.CompilerParams(dimension_semantics=("arbitrary",), vmem_limit_bytes=VMEM_LIMIT),
        name="proj",
    )(x2, g1, w_cat, b_gate, cos_t, sin_t, ln_g, ln_b)


def _attn_kernel(lam_ref, q_ref, k_ref, v_ref, sg_ref, o_ref, acc_ref, m_ref, l_ref, *, tq, tk):
    i = pl.program_id(2)
    q = q_ref[0]
    lane = lax.broadcasted_iota(I32, (tq, DA_V_DIM), 1)
    zero = jnp.zeros_like(q)
    qs = (jnp.where(lane < DA_HEAD_DIM, q, zero), jnp.where(lane >= DA_HEAD_DIM, q, zero))

    m_ref[...] = jnp.full(m_ref.shape, -1e30, F32)
    l_ref[...] = jnp.zeros(l_ref.shape, F32)
    acc_ref[...] = jnp.zeros(acc_ref.shape, F32)

    def step(j, masked):
        start = pl.multiple_of(j * tk, tk)
        kb = k_ref[0, pl.ds(start, tk), :]
        vb = v_ref[0, pl.ds(start, tk), :]
        if masked:
            rowc = (i * tq + lax.broadcasted_iota(I32, (tq, tk), 0)) // CHUNK
            colc = (j * tk + lax.broadcasted_iota(I32, (tq, tk), 1)) // CHUNK
            visible = colc <= rowc
        for c in range(2):
            s = _dot_t(qs[c], kb)
            if masked:
                s = jnp.where(visible, s, -jnp.inf)
            m_old = m_ref[c]
            m_new = jnp.maximum(m_old, jnp.max(s, axis=-1, keepdims=True))
            alpha = jnp.exp(m_old - m_new)
            p = jnp.exp(s - m_new)
            l_ref[c] = alpha * l_ref[c] + jnp.sum(p, axis=-1, keepdims=True)
            acc_ref[c] = alpha * acc_ref[c] + _dot(p.astype(BF16), vb)
            m_ref[c] = m_new

    nfull = i * (tq // tk)

    def body(j, carry):
        step(j, False)
        return carry

    lax.fori_loop(0, nfull, body, 0)
    for jj in range(tq // tk):
        step(nfull + jj, True)

    lam = (jnp.exp(jnp.sum(lam_ref[0:1, :] * lam_ref[1:2, :], axis=-1, keepdims=True))
           - jnp.exp(jnp.sum(lam_ref[2:3, :] * lam_ref[3:4, :], axis=-1, keepdims=True)) + LAMBDA_INIT)
    o = acc_ref[0] / l_ref[0] - lam * (acc_ref[1] / l_ref[1])
    o_ref[0] = (_rms(o, sg_ref[...]) * (1.0 - LAMBDA_INIT)).astype(BF16)


def _attn_call(lam4, q3, k3, v3, subln_g, tq, tk):
    b, s, _ = q3.shape
    kern = functools.partial(_attn_kernel, tq=tq, tk=tk)
    return pl.pallas_call(
        kern,
        grid=(b, DA_HEADS, s // tq),
        in_specs=[
            pl.BlockSpec((4, DA_HEAD_DIM), lambda bi, h, i: (0, 0)),
            pl.BlockSpec((1, tq, DA_V_DIM), lambda bi, h, i: (bi, i, h)),
            pl.BlockSpec((1, s, DA_V_DIM), lambda bi, h, i: (bi, 0, h)),
            pl.BlockSpec((1, s, DA_V_DIM), lambda bi, h, i: (bi, 0, h)),
            pl.BlockSpec((1, DA_V_DIM), lambda bi, h, i: (0, 0)),
        ],
        out_specs=pl.BlockSpec((1, tq, DA_V_DIM), lambda bi, h, i: (bi, i, h)),
        out_shape=jax.ShapeDtypeStruct((b, s, DA_WIDTH), BF16),
        scratch_shapes=[
            pltpu.VMEM((2, tq, DA_V_DIM), F32),
            pltpu.VMEM((2, tq, 1), F32),
            pltpu.VMEM((2, tq, 1), F32),
        ],
        compiler_params=pltpu.CompilerParams(dimension_semantics=("arbitrary",) * 3, vmem_limit_bytes=VMEM_LIMIT),
        name="attn",
    )(lam4, q3, k3, v3, subln_g)


def _mix_kernel(x_ref, u_ref, sv_ref, gate_ref, ya_ref, ws_ref, bs_ref, wa_ref, wb_ref, wo_ref, g2_ref,
                wq_ref, sk_ref, x1_ref, h2_ref, sc_ref):
    tm = x_ref.shape[0]
    pc = lax.broadcasted_iota(I32, (SG_BLOCK, SG_BLOCK), 0) // CHUNK
    pr = lax.broadcasted_iota(I32, (SG_BLOCK, SG_BLOCK), 1) // CHUNK
    causal = pr <= pc
    rows = []
    for n in range(tm // SG_BLOCK):
        cols = []
        for g in range(SG_GROUPS):
            w = jnp.where(causal, ws_ref[g], 0.0).astype(BF16)
            vb = sv_ref[n * SG_BLOCK:(n + 1) * SG_BLOCK, g * SG_GROUP_DIM:(g + 1) * SG_GROUP_DIM]
            cols.append(_dot(w, vb))
        rows.append(jnp.concatenate(cols, axis=1) + bs_ref[...])
    s = jnp.concatenate(rows, axis=0)
    yb = (u_ref[...].astype(F32) * s).astype(BF16)
    merged = (gate_ref[:, :D_MODEL].astype(F32) * _dot(ya_ref[...], wa_ref[...])
              + gate_ref[:, D_MODEL:].astype(F32) * _dot(yb, wb_ref[...]))
    x1 = x_ref[...] + _dot(merged.astype(BF16), wo_ref[...])
    x1_ref[...] = x1
    h2 = _rms(x1, g2_ref[...]).astype(BF16)
    h2_ref[...] = h2
    qp = _dot(h2, wq_ref[...]).astype(BF16)
    for hp in range(2 * PEER_HEADS):
        sc_ref[hp] = _dot_t(sk_ref[hp], qp[:, hp * PEER_HALF:(hp + 1) * PEER_HALF])


def _mix_call(x2, u_act, sv_ln, gates, ya, w_s, b_s_full, w_a, w_b, w_o, g2, w_q, sk, tm):
    t = x2.shape[0]
    row = lambda i: (i, 0)
    fixed2 = lambda i: (0, 0)
    fixed3 = lambda i: (0, 0, 0)
    return pl.pallas_call(
        _mix_kernel,
        grid=(t // tm,),
        in_specs=[
            pl.BlockSpec((tm, D_MODEL), row),
            pl.BlockSpec((tm, SG_WIDTH), row),
            pl.BlockSpec((tm, SG_WIDTH), row),
            pl.BlockSpec((tm, 2 * D_MODEL), row),
            pl.BlockSpec((tm, DA_WIDTH), row),
            pl.BlockSpec((SG_GROUPS, SG_BLOCK, SG_BLOCK), fixed3),
            pl.BlockSpec((SG_BLOCK, SG_WIDTH), fixed2),
            pl.BlockSpec((DA_WIDTH, D_MODEL), fixed2),
            pl.BlockSpec((SG_WIDTH, D_MODEL), fixed2),
            pl.BlockSpec((D_MODEL, D_MODEL), fixed2),
            pl.BlockSpec((1, D_MODEL), fixed2),
            pl.BlockSpec((D_MODEL, 2 * PEER_HEADS * PEER_HALF), fixed2),
            pl.BlockSpec((2 * PEER_HEADS, PEER_NKEYS, PEER_HALF), fixed3),
        ],
        out_specs=[
            pl.BlockSpec((tm, D_MODEL), row),
            pl.BlockSpec((tm, D_MODEL), row),
            pl.BlockSpec((2 * PEER_HEADS, PEER_NKEYS, tm), lambda i: (0, 0, i)),
        ],
        out_shape=[
            jax.ShapeDtypeStruct((t, D_MODEL), F32),
            jax.ShapeDtypeStruct((t, D_MODEL), BF16),
            jax.ShapeDtypeStruct((2 * PEER_HEADS, PEER_NKEYS, t), F32),
        ],
        compiler_params=pltpu.CompilerParams(dimension_semantics=("arbitrary",), vmem_limit_bytes=VMEM_LIMIT),
        name="mix",
    )(x2, u_act, sv_ln, gates, ya, w_s, b_s_full, w_a, w_b, w_o, g2, w_q, sk)


def _top16(vals, n):
    tt = vals.shape[1]
    pos = lax.broadcasted_iota(I32, (n, tt), 0)
    cur = vals
    out_v, out_i = [], []
    for _ in range(PEER_TOPK):
        m = jnp.max(cur, axis=0, keepdims=True)
        idx = jnp.min(jnp.where(cur == m, pos, n), axis=0, keepdims=True)
        out_v.append(m)
        out_i.append(idx)
        cur = jnp.where(pos == idx, -jnp.inf, cur)
    return out_v, out_i


def _topk_kernel(sc_ref, row_ref, g0_ref, g1_ref):
    tt = sc_ref.shape[2]
    k = PEER_TOPK
    rows_all, g0_all, g1_all = [], [], []
    for h in range(PEER_HEADS):
        s1, i1 = _top16(sc_ref[2 * h], PEER_NKEYS)
        s2, i2 = _top16(sc_ref[2 * h + 1], PEER_NKEYS)
        s2m = jnp.concatenate(s2, axis=0)
        i2m = jnp.concatenate(i2, axis=0)
        cand = jnp.concatenate([s1[a] + s2m for a in range(k)], axis=0)
        cidx = jnp.concatenate([i1[a] * PEER_NKEYS + i2m for a in range(k)], axis=0)
        top_s, pos = _top16(cand, k * k)
        cpos = lax.broadcasted_iota(I32, (k * k, tt), 0)
        e = [jnp.sum(jnp.where(cpos == pos[r], cidx, 0), axis=0, keepdims=True) for r in range(k)]
        w = [jnp.exp(top_s[r] - top_s[0]) for r in range(k)]
        den = w[0]
        for r in range(1, k):
            den = den + w[r]
        em = jnp.concatenate(e, axis=0)
        gm = jnp.concatenate(w, axis=0) / den
        upper = em >= PEER_PAIR_ROWS
        rows_all.append(jnp.where(upper, em - PEER_PAIR_ROWS, em) * D_CHUNKS)
        g0_all.append(jnp.where(upper, 0.0, gm))
        g1_all.append(jnp.where(upper, gm, 0.0))
    row_ref[...] = jnp.concatenate(rows_all, axis=0).T
    g0_ref[...] = jnp.concatenate(g0_all, axis=0).T
    g1_ref[...] = jnp.concatenate(g1_all, axis=0).T


def _topk_call(sc, tt):
    t = sc.shape[2]
    blk = pl.BlockSpec((tt, PEER_SLOTS), lambda i: (i, 0))
    return pl.pallas_call(
        _topk_kernel,
        grid=(t // tt,),
        in_specs=[pl.BlockSpec((2 * PEER_HEADS, PEER_NKEYS, tt), lambda i: (0, 0, i))],
        out_specs=[blk, blk, blk],
        out_shape=[jax.ShapeDtypeStruct((t, PEER_SLOTS), I32),
                   jax.ShapeDtypeStruct((t, PEER_SLOTS), F32),
                   jax.ShapeDtypeStruct((t, PEER_SLOTS), F32)],
        compiler_params=pltpu.CompilerParams(dimension_semantics=("arbitrary",), vmem_limit_bytes=VMEM_LIMIT),
        name="topk",
    )(sc)


def _gather_rows(row_ref, tab_ref, tile_ref, t):
    for j in range(PEER_SLOTS):
        r = pl.multiple_of(row_ref[t, j], D_CHUNKS)
        tile_ref[pl.ds(j, D_CHUNKS, stride=GATHER_STRIDE), :] = tab_ref[pl.ds(r, D_CHUNKS), :]


def _chunk_operand(tile_ref, c):
    return pltpu.bitcast(tile_ref[pl.ds(c * GATHER_STRIDE, PEER_SLOTS), :], BF16)


def _gather_group(row_ref, tab_ref, tiles, g, n_groups):
    t0 = jnp.minimum(g, n_groups - 1) * PEER_GROUP
    for s in range(PEER_GROUP):
        _gather_rows(row_ref, tab_ref, tiles[s], t0 + s)


def _pipelined_groups(row_ref, tab_ref, tiles, n_groups, compute):
    _gather_group(row_ref, tab_ref, tiles, 0, n_groups)

    def body(g, carry):
        compute(g)
        _gather_group(row_ref, tab_ref, tiles, g + 1, n_groups)
        return carry

    lax.fori_loop(0, n_groups, body, 0)


def _peer_u_kernel(row_ref, h2_ref, g0_ref, g1_ref, tab_ref, coef_ref, *scratch):
    tiles, acc_ref = scratch[:PEER_GROUP], scratch[PEER_GROUP]
    tt = h2_ref.shape[0]
    sub = lax.broadcasted_iota(I32, (PACK_ROWS, 2 * PEER_SLOTS), 0)
    acc_ref[...] = jnp.zeros(acc_ref.shape, F32)

    def compute(g):
        t0 = g * PEER_GROUP
        base = pl.multiple_of((t0 // PACK_ROWS) * PACK_ROWS, PACK_ROWS)
        r0 = t0 - base
        xg = h2_ref[pl.ds(base, PACK_ROWS), :]
        out = acc_ref[pl.ds(base, PACK_ROWS), :]
        for s in range(PEER_GROUP):
            res = _dot_t(xg[:, 0:LANES], _chunk_operand(tiles[s], 0))
            for c in range(1, D_CHUNKS):
                res = res + _dot_t(xg[:, c * LANES:(c + 1) * LANES], _chunk_operand(tiles[s], c))
            out = jnp.where(sub == r0 + s, res, out)
        acc_ref[pl.ds(base, PACK_ROWS), :] = out

    _pipelined_groups(row_ref, tab_ref, tiles, tt // PEER_GROUP, compute)
    src = lax.broadcasted_iota(I32, (PEER_SLOTS, 2 * PEER_SLOTS), 0)
    dst = lax.broadcasted_iota(I32, (PEER_SLOTS, 2 * PEER_SLOTS), 1)
    e0 = jnp.where(dst == 2 * src, 1.0, 0.0).astype(BF16)
    e1 = jnp.where(dst == 2 * src + 1, 1.0, 0.0).astype(BF16)
    w2 = _dot(g0_ref[...].astype(BF16), e0) + _dot(g1_ref[...].astype(BF16), e1)
    coef_ref[...] = (_gelu(acc_ref[...]) * w2).astype(BF16)


def _tile_scratch():
    return [pltpu.VMEM((D_CHUNKS * GATHER_STRIDE, LANES), I32) for _ in range(PEER_GROUP)]


def _peer_u_call(rows, h2, g0, g1, u_tab, tt):
    t = h2.shape[0]
    tok = lambda i: (i, 0)
    return pl.pallas_call(
        _peer_u_kernel,
        grid=(t // tt,),
        in_specs=[
            pl.BlockSpec((tt, PEER_SLOTS), tok, memory_space=pltpu.SMEM),
            pl.BlockSpec((tt, D_MODEL), tok),
            pl.BlockSpec((tt, PEER_SLOTS), tok),
            pl.BlockSpec((tt, PEER_SLOTS), tok),
            pl.BlockSpec(u_tab.shape, lambda i: (0, 0), pipeline_mode=pl.Buffered(1)),
        ],
        out_specs=pl.BlockSpec((tt, 2 * PEER_SLOTS), tok),
        out_shape=jax.ShapeDtypeStruct((t, 2 * PEER_SLOTS), BF16),
        scratch_shapes=_tile_scratch() + [pltpu.VMEM((tt, 2 * PEER_SLOTS), F32)],
        compiler_params=pltpu.CompilerParams(dimension_semantics=("arbitrary",),
                                             vmem_limit_bytes=VMEM_LIMIT_TABLE),
        name="peer_u",
    )(rows, h2, g0, g1, u_tab)


def _peer_v_kernel(row_ref, coef_ref, x1_ref, gf_ref, tab_ref, out_ref, *scratch):
    tiles, acc_ref = scratch[:PEER_GROUP], scratch[PEER_GROUP]
    tt = coef_ref.shape[0]
    sub = lax.broadcasted_iota(I32, (PACK_ROWS, LANES), 0)
    acc_ref[...] = jnp.zeros(acc_ref.shape, F32)

    def compute(g):
        t0 = g * PEER_GROUP
        base = pl.multiple_of((t0 // PACK_ROWS) * PACK_ROWS, PACK_ROWS)
        r0 = t0 - base
        cg = coef_ref[pl.ds(base, PACK_ROWS), :]
        outs = [acc_ref[pl.ds(base, PACK_ROWS), c * LANES:(c + 1) * LANES] for c in range(D_CHUNKS)]
        for s in range(PEER_GROUP):
            for c in range(D_CHUNKS):
                outs[c] = jnp.where(sub == r0 + s, _dot(cg, _chunk_operand(tiles[s], c)), outs[c])
        for c in range(D_CHUNKS):
            acc_ref[pl.ds(base, PACK_ROWS), c * LANES:(c + 1) * LANES] = outs[c]

    _pipelined_groups(row_ref, tab_ref, tiles, tt // PEER_GROUP, compute)
    out_ref[...] = _rms(x1_ref[...] + acc_ref[...], gf_ref[...])


def _peer_v_call(rows, coef, x1, gf, v_tab, tt):
    t = x1.shape[0]
    tok = lambda i: (i, 0)
    return pl.pallas_call(
        _peer_v_kernel,
        grid=(t // tt,),
        in_specs=[
            pl.BlockSpec((tt, PEER_SLOTS), tok, memory_space=pltpu.SMEM),
            pl.BlockSpec((tt, 2 * PEER_SLOTS), tok),
            pl.BlockSpec((tt, D_MODEL), tok),
            pl.BlockSpec((1, D_MODEL), lambda i: (0, 0)),
            pl.BlockSpec(v_tab.shape, lambda i: (0, 0), pipeline_mode=pl.Buffered(1)),
        ],
        out_specs=pl.BlockSpec((tt, D_MODEL), tok),
        out_shape=jax.ShapeDtypeStruct((t, D_MODEL), F32),
        scratch_shapes=_tile_scratch() + [pltpu.VMEM((tt, D_MODEL), F32)],
        compiler_params=pltpu.CompilerParams(dimension_semantics=("arbitrary",),
                                             vmem_limit_bytes=VMEM_LIMIT_TABLE),
        name="peer_v",
    )(rows, coef, x1, gf, v_tab)


def _pack_pairs(tab):
    bits = lax.bitcast_convert_type(tab.astype(BF16), jnp.uint16).astype(jnp.uint32)
    word = bits[:PEER_PAIR_ROWS] | (bits[PEER_PAIR_ROWS:] << 16)
    return lax.bitcast_convert_type(word, I32).reshape(PEER_PAIR_ROWS * D_CHUNKS, LANES)


def _rope_tables(seq):
    half = DA_HEAD_DIM // 2
    inv = 1.0 / (ROPE_THETA ** (jnp.arange(0, DA_HEAD_DIM, 2, dtype=F32) / DA_HEAD_DIM))
    ang = jnp.arange(seq, dtype=F32)[:, None] * inv[None, :]
    cos, sin = jnp.cos(ang), jnp.sin(ang)
    reps = LANES // DA_HEAD_DIM
    cos_t = jnp.tile(jnp.concatenate([cos, cos], axis=1), (1, reps))
    sin_t = jnp.tile(jnp.concatenate([-sin, sin], axis=1), (1, reps))
    del half
    return cos_t, sin_t


def kernel(x, norm1_g, w_in, lambda_q1, lambda_k1, lambda_q2, lambda_k2, subln_g, sgu_ln_g, sgu_ln_b, w_spatial, b_spatial, w_branch_a, w_branch_b, w_gate, b_gate, w_out, norm2_g, peer_wq, peer_subkeys, peer_u, peer_v, norm_f_g):
    b, s, d = x.shape
    t = b * s
    l = 0
    x2 = x.reshape(t, d)
    cos_t, sin_t = _rope_tables(s)
    w_cat = jnp.concatenate([w_in[l], w_gate[l]], axis=1).astype(BF16)

    q, k, v, u_act, sv_ln, gates = _proj_call(
        x2, norm1_g[l][None], w_cat, b_gate[l][None], cos_t, sin_t, sgu_ln_g[l][None], sgu_ln_b[l][None],
        seq=s, tm=min(PROJ_TM, s))

    lam4 = jnp.stack([lambda_q1[l], lambda_k1[l], lambda_q2[l], lambda_k2[l]]).astype(F32)
    ya = _attn_call(lam4, q.reshape(b, s, DA_WIDTH), k.reshape(b, s, DA_WIDTH), v.reshape(b, s, DA_WIDTH),
                    subln_g[l][None], tq=min(ATTN_TQ, s), tk=min(ATTN_TK, s)).reshape(t, DA_WIDTH)

    b_s_full = jnp.repeat(b_spatial[l].T, SG_GROUP_DIM, axis=1)
    sk = peer_subkeys[l].reshape(2 * PEER_HEADS, PEER_NKEYS, PEER_HALF).astype(BF16)
    x1, h2, sc = _mix_call(x2, u_act, sv_ln, gates, ya, w_spatial[l], b_s_full,
                           w_branch_a[l].astype(BF16), w_branch_b[l].astype(BF16), w_out[l].astype(BF16),
                           norm2_g[l][None], peer_wq[l].astype(BF16), sk, tm=min(MIX_TM, t))

    rows, g0, g1 = _topk_call(sc, tt=min(TOPK_TT, t))
    coef = _peer_u_call(rows, h2, g0, g1, _pack_pairs(peer_u[l]), tt=min(PEER_TT, t))
    out = _peer_v_call(rows, coef, x1, norm_f_g[None], _pack_pairs(peer_v[l]), tt=min(PEER_TT, t))
    return out.reshape(b, s, d)
```

```python
import functools
import math

import jax
import jax.numpy as jnp
from jax import lax
from jax.experimental import pallas as pl
from jax.experimental.pallas import tpu as pltpu

F32 = jnp.float32
BF16 = jnp.bfloat16
I32 = jnp.int32

D_MODEL = 1024
CHUNK = 64
ROPE_THETA = 10000.0
EPS = 1e-6

DA_HEADS = 4
DA_HEAD_DIM = 64
DA_V_DIM = 2 * DA_HEAD_DIM
DA_WIDTH = DA_HEADS * DA_V_DIM

SG_GROUPS = 4
SG_BLOCK = 128
SG_GROUP_DIM = 128
SG_WIDTH = SG_GROUPS * SG_GROUP_DIM

PEER_HEADS = 8
PEER_NKEYS = 128
PEER_HALF = 128
PEER_TOPK = 16
PEER_SLOTS = PEER_HEADS * PEER_TOPK
PEER_N_EXPERTS = PEER_NKEYS * PEER_NKEYS
PEER_PAIR_ROWS = PEER_N_EXPERTS // 2

LANES = 128
SUBLANES = 8
PACK_ROWS = 16
PEER_GROUP = 8
D_CHUNKS = D_MODEL // LANES
GATHER_STRIDE = PEER_SLOTS + 1

ATTN_TQ = 512
ATTN_TK = 512
MIX_TM = 256
TOPK_TT = 256
PEER_TT = 128

VMEM_LIMIT = 48 * 1024 * 1024
VMEM_LIMIT_TABLE = 56 * 1024 * 1024

LAMBDA_INIT = 0.8 - 0.6 * math.exp(-0.3 * 0)


def _gelu(x):
    return 0.5 * x * (1.0 + lax.erf(x * (0.5 ** 0.5)))


def _rms(x, g):
    return x * lax.rsqrt(jnp.mean(x * x, axis=-1, keepdims=True) + EPS) * g


def _dot(a, b):
    return jnp.dot(a, b, preferred_element_type=F32)


def _dot_t(a, b):
    return lax.dot_general(a, b, (((1,), (1,)), ((), ())), preferred_element_type=F32)


def _proj_kernel(x_ref, g1_ref, w_ref, bg_ref, cos_ref, sin_ref, lng_ref, lnb_ref,
                 q_ref, k0_ref, k1_ref, vt_ref, u_ref, sv_ref, gate_ref):
    tm = x_ref.shape[0]
    hb = _rms(x_ref[...], g1_ref[...]).astype(BF16)
    cos = cos_ref[...]
    sin = sin_ref[...]
    lane = lax.broadcasted_iota(I32, (tm, LANES), 1)
    first_half = (lane % DA_HEAD_DIM) < (DA_HEAD_DIM // 2)

    def rope(t):
        outs = []
        for b in range(DA_WIDTH // LANES):
            blk = t[:, b * LANES:(b + 1) * LANES]
            swapped = jnp.where(first_half, pltpu.roll(blk, LANES - DA_HEAD_DIM // 2, 1),
                                pltpu.roll(blk, DA_HEAD_DIM // 2, 1))
            outs.append(blk * cos + swapped * sin)
        return jnp.concatenate(outs, axis=1)

    o = 0
    q = _dot(hb, w_ref[:, o:o + DA_WIDTH]); o += DA_WIDTH
    q_ref[...] = (rope(q) * (DA_HEAD_DIM ** -0.5)).astype(BF16)
    k = rope(_dot(hb, w_ref[:, o:o + DA_WIDTH])); o += DA_WIDTH
    comp0 = (lax.broadcasted_iota(I32, k.shape, 1) % DA_V_DIM) < DA_HEAD_DIM
    k0_ref[...] = jnp.where(comp0, k, 0.0).astype(BF16)
    k1_ref[...] = jnp.where(comp0, 0.0, k).astype(BF16)
    vt_ref[0] = _dot(hb, w_ref[:, o:o + DA_WIDTH]).T.astype(BF16); o += DA_WIDTH
    u_ref[...] = _gelu(_dot(hb, w_ref[:, o:o + SG_WIDTH])).astype(BF16); o += SG_WIDTH
    sv = _gelu(_dot(hb, w_ref[:, o:o + SG_WIDTH])); o += SG_WIDTH
    mu = jnp.mean(sv, axis=-1, keepdims=True)
    svc = sv - mu
    sv_ref[...] = (svc * lax.rsqrt(jnp.mean(svc * svc, axis=-1, keepdims=True) + EPS) * lng_ref[...]
                   + lnb_ref[...]).astype(BF16)
    gate_ref[...] = jax.nn.sigmoid(_dot(hb, w_ref[:, o:o + 2 * D_MODEL]) + bg_ref[...]).astype(BF16)


def _proj_call(x2, g1, w_cat, b_gate, cos_t, sin_t, ln_g, ln_b, seq, tm):
    t = x2.shape[0]
    nseq = seq // tm
    row = lambda i: (i, 0)
    fixed = lambda i: (0, 0)
    pos = lambda i: (i % nseq, 0)
    wcols = w_cat.shape[1]
    out_shapes = [jax.ShapeDtypeStruct((t, DA_WIDTH), BF16)] * 3 + \
                 [jax.ShapeDtypeStruct((t // tm, DA_WIDTH, tm), BF16)] + \
                 [jax.ShapeDtypeStruct((t, SG_WIDTH), BF16)] * 2 + \
                 [jax.ShapeDtypeStruct((t, 2 * D_MODEL), BF16)]
    return pl.pallas_call(
        _proj_kernel,
        grid=(t // tm,),
        in_specs=[
            pl.BlockSpec((tm, D_MODEL), row),
            pl.BlockSpec((1, D_MODEL), fixed),
            pl.BlockSpec((D_MODEL, wcols), fixed),
            pl.BlockSpec((1, 2 * D_MODEL), fixed),
            pl.BlockSpec((tm, LANES), pos),
            pl.BlockSpec((tm, LANES), pos),
            pl.BlockSpec((1, SG_WIDTH), fixed),
            pl.BlockSpec((1, SG_WIDTH), fixed),
        ],
        out_specs=[pl.BlockSpec((tm, DA_WIDTH), row)] * 3 +
                  [pl.BlockSpec((1, DA_WIDTH, tm), lambda i: (i, 0, 0))] +
                  [pl.BlockSpec((tm, SG_WIDTH), row)] * 2 +
                  [pl.BlockSpec((tm, 2 * D_MODEL), row)],
        out_shape=out_shapes,
        compiler_params=pltpu.CompilerParams(dimension_semantics=("arbitrary",), vmem_limit_bytes=VMEM_LIMIT),
        name="proj",
    )(x2, g1, w_cat, b_gate, cos_t, sin_t, ln_g, ln_b)


def _attn_kernel(lam_ref, q_ref, k0_ref, k1_ref, vt_ref, sg_ref, o_ref, acc_ref, m_ref, l_ref, *, tq, tk):
    i = pl.program_id(2)
    q = q_ref[...]

    m_ref[...] = jnp.full(m_ref.shape, -1e30, F32)
    l_ref[...] = jnp.zeros(l_ref.shape, F32)
    acc_ref[...] = jnp.zeros(acc_ref.shape, F32)

    def step(j, masked):
        start = pl.multiple_of(j * tk, tk)
        vt = vt_ref[j]
        if masked:
            keyc = (j * tk + lax.broadcasted_iota(I32, (tk, tq), 0)) // CHUNK
            qryc = (i * tq + lax.broadcasted_iota(I32, (tk, tq), 1)) // CHUNK
            visible = keyc <= qryc
        for c, k_ref in enumerate((k0_ref, k1_ref)):
            st = _dot_t(k_ref[pl.ds(start, tk), :], q)
            if masked:
                st = jnp.where(visible, st, -jnp.inf)
            m_old = m_ref[c]
            m_new = jnp.maximum(m_old, jnp.max(st, axis=0, keepdims=True))
            alpha = jnp.exp(m_old - m_new)
            p = jnp.exp(st - m_new)
            l_ref[c] = alpha * l_ref[c] + jnp.sum(p, axis=0, keepdims=True)
            acc_ref[c] = alpha * acc_ref[c] + _dot(vt, p.astype(BF16))
            m_ref[c] = m_new

    nfull = i * (tq // tk)

    def body(j, carry):
        step(j, False)
        return carry

    lax.fori_loop(0, nfull, body, 0)
    for jj in range(tq // tk):
        step(nfull + jj, True)

    lam = (jnp.exp(jnp.sum(lam_ref[0:1, :] * lam_ref[1:2, :], axis=-1, keepdims=True))
           - jnp.exp(jnp.sum(lam_ref[2:3, :] * lam_ref[3:4, :], axis=-1, keepdims=True)) + LAMBDA_INIT)
    o = acc_ref[0] / l_ref[0] - lam * (acc_ref[1] / l_ref[1])
    o = o * lax.rsqrt(jnp.mean(o * o, axis=0, keepdims=True) + EPS)
    o_ref[...] = (o.T * sg_ref[...] * (1.0 - LAMBDA_INIT)).astype(BF16)


def _attn_call(lam4, q, k0, k1, vt, subln_g, batch, tq, tk):
    t = q.shape[0]
    s = t // batch
    nq, nk = s // tq, s // tk
    kern = functools.partial(_attn_kernel, tq=tq, tk=tk)
    qblk = pl.BlockSpec((tq, DA_V_DIM), lambda bi, h, i: (bi * nq + i, h))
    kblk = pl.BlockSpec((s, DA_V_DIM), lambda bi, h, i: (bi, h))
    return pl.pallas_call(
        kern,
        grid=(batch, DA_HEADS, nq),
        in_specs=[
            pl.BlockSpec((4, DA_HEAD_DIM), lambda bi, h, i: (0, 0)),
            qblk, kblk, kblk,
            pl.BlockSpec((nk, DA_V_DIM, tk), lambda bi, h, i: (bi, h, 0)),
            pl.BlockSpec((1, DA_V_DIM), lambda bi, h, i: (0, 0)),
        ],
        out_specs=qblk,
        out_shape=jax.ShapeDtypeStruct((t, DA_WIDTH), BF16),
        scratch_shapes=[
            pltpu.VMEM((2, DA_V_DIM, tq), F32),
            pltpu.VMEM((2, 1, tq), F32),
            pltpu.VMEM((2, 1, tq), F32),
        ],
        compiler_params=pltpu.CompilerParams(dimension_semantics=("arbitrary",) * 3, vmem_limit_bytes=VMEM_LIMIT),
        name="attn",
    )(lam4, q, k0, k1, vt, subln_g)


def _mix_kernel(x_ref, u_ref, sv_ref, gate_ref, ya_ref, ws_ref, bs_ref, wa_ref, wb_ref, wo_ref, g2_ref,
                wq_ref, sk_ref, x1_ref, h2_ref, sc_ref):
    tm = x_ref.shape[0]
    pc = lax.broadcasted_iota(I32, (SG_BLOCK, SG_BLOCK), 0) // CHUNK
    pr = lax.broadcasted_iota(I32, (SG_BLOCK, SG_BLOCK), 1) // CHUNK
    causal = pr <= pc
    rows = []
    for n in range(tm // SG_BLOCK):
        cols = []
        for g in range(SG_GROUPS):
            w = jnp.where(causal, ws_ref[g], 0.0).astype(BF16)
            vb = sv_ref[n * SG_BLOCK:(n + 1) * SG_BLOCK, g * SG_GROUP_DIM:(g + 1) * SG_GROUP_DIM]
            cols.append(_dot(w, vb))
        rows.append(jnp.concatenate(cols, axis=1) + bs_ref[...])
    s = jnp.concatenate(rows, axis=0)
    yb = (u_ref[...].astype(F32) * s).astype(BF16)
    merged = (gate_ref[:, :D_MODEL].astype(F32) * _dot(ya_ref[...], wa_ref[...])
              + gate_ref[:, D_MODEL:].astype(F32) * _dot(yb, wb_ref[...]))
    x1 = x_ref[...] + _dot(merged.astype(BF16), wo_ref[...])
    x1_ref[...] = x1
    h2 = _rms(x1, g2_ref[...]).astype(BF16)
    h2_ref[...] = h2
    qp = _dot(h2, wq_ref[...]).astype(BF16)
    for hp in range(2 * PEER_HEADS):
        sc_ref[hp] = _dot_t(sk_ref[hp], qp[:, hp * PEER_HALF:(hp + 1) * PEER_HALF])


def _mix_call(x2, u_act, sv_ln, gates, ya, w_s, b_s_full, w_a, w_b, w_o, g2, w_q, sk, tm):
    t = x2.shape[0]
    row = lambda i: (i, 0)
    fixed2 = lambda i: (0, 0)
    fixed3 = lambda i: (0, 0, 0)
    return pl.pallas_call(
        _mix_kernel,
        grid=(t // tm,),
        in_specs=[
            pl.BlockSpec((tm, D_MODEL), row),
            pl.BlockSpec((tm, SG_WIDTH), row),
            pl.BlockSpec((tm, SG_WIDTH), row),
            pl.BlockSpec((tm, 2 * D_MODEL), row),
            pl.BlockSpec((tm, DA_WIDTH), row),
            pl.BlockSpec((SG_GROUPS, SG_BLOCK, SG_BLOCK), fixed3),
            pl.BlockSpec((SG_BLOCK, SG_WIDTH), fixed2),
            pl.BlockSpec((DA_WIDTH, D_MODEL), fixed2),
            pl.BlockSpec((SG_WIDTH, D_MODEL), fixed2),
            pl.BlockSpec((D_MODEL, D_MODEL), fixed2),
            pl.BlockSpec((1, D_MODEL), fixed2),
            pl.BlockSpec((D_MODEL, 2 * PEER_HEADS * PEER_HALF), fixed2),
            pl.BlockSpec((2 * PEER_HEADS, PEER_NKEYS, PEER_HALF), fixed3),
        ],
        out_specs=[
            pl.BlockSpec((tm, D_MODEL), row),
            pl.BlockSpec((tm, D_MODEL), row),
            pl.BlockSpec((2 * PEER_HEADS, PEER_NKEYS, tm), lambda i: (0, 0, i)),
        ],
        out_shape=[
            jax.ShapeDtypeStruct((t, D_MODEL), F32),
            jax.ShapeDtypeStruct((t, D_MODEL), BF16),
            jax.ShapeDtypeStruct((2 * PEER_HEADS, PEER_NKEYS, t), F32),
        ],
        compiler_params=pltpu.CompilerParams(dimension_semantics=("arbitrary",), vmem_limit_bytes=VMEM_LIMIT),
        name="mix",
    )(x2, u_act, sv_ln, gates, ya, w_s, b_s_full, w_a, w_b, w_o, g2, w_q, sk)


def _top16(vals, n):
    tt = vals.shape[1]
    pos = lax.broadcasted_iota(I32, (n, tt), 0)
    cur = vals
    out_v, out_i = [], []
    for _ in range(PEER_TOPK):
        m = jnp.max(cur, axis=0, keepdims=True)
        idx = jnp.min(jnp.where(cur == m, pos, n), axis=0, keepdims=True)
        out_v.append(m)
        out_i.append(idx)
        cur = jnp.where(pos == idx, -jnp.inf, cur)
    return out_v, out_i


def _topk_kernel(sc_ref, row_ref, g0_ref, g1_ref):
    tt = sc_ref.shape[2]
    k = PEER_TOPK
    rows_all, g0_all, g1_all = [], [], []
    for h in range(PEER_HEADS):
        s1, i1 = _top16(sc_ref[2 * h], PEER_NKEYS)
        s2, i2 = _top16(sc_ref[2 * h + 1], PEER_NKEYS)
        s2m = jnp.concatenate(s2, axis=0)
        i2m = jnp.concatenate(i2, axis=0)
        keep = [k // (a + 1) for a in range(k)]
        n_cand = -(-sum(keep) // SUBLANES) * SUBLANES
        pad = n_cand - sum(keep)
        cand = jnp.concatenate([s1[a] + s2m[:keep[a]] for a in range(k)]
                               + [jnp.full((pad, tt), -jnp.inf, F32)], axis=0)
        cidx = jnp.concatenate([i1[a] * PEER_NKEYS + i2m[:keep[a]] for a in range(k)]
                               + [jnp.zeros((pad, tt), I32)], axis=0)
        top_s, pos = _top16(cand, n_cand)
        cpos = lax.broadcasted_iota(I32, (n_cand, tt), 0)
        e = [jnp.sum(jnp.where(cpos == pos[r], cidx, 0), axis=0, keepdims=True) for r in range(k)]
        w = [jnp.exp(top_s[r] - top_s[0]) for r in range(k)]
        den = w[0]
        for r in range(1, k):
            den = den + w[r]
        em = jnp.concatenate(e, axis=0)
        gm = jnp.concatenate(w, axis=0) / den
        upper = em >= PEER_PAIR_ROWS
        rows_all.append(jnp.where(upper, em - PEER_PAIR_ROWS, em) * D_CHUNKS)
        g0_all.append(jnp.where(upper, 0.0, gm))
        g1_all.append(jnp.where(upper, gm, 0.0))
    row_ref[...] = jnp.concatenate(rows_all, axis=0).T
    g0_ref[...] = jnp.concatenate(g0_all, axis=0).T
    g1_ref[...] = jnp.concatenate(g1_all, axis=0).T


def _topk_call(sc, tt):
    t = sc.shape[2]
    blk = pl.BlockSpec((tt, PEER_SLOTS), lambda i: (i, 0))
    return pl.pallas_call(
        _topk_kernel,
        grid=(t // tt,),
        in_specs=[pl.BlockSpec((2 * PEER_HEADS, PEER_NKEYS, tt), lambda i: (0, 0, i))],
        out_specs=[blk, blk, blk],
        out_shape=[jax.ShapeDtypeStruct((t, PEER_SLOTS), I32),
                   jax.ShapeDtypeStruct((t, PEER_SLOTS), F32),
                   jax.ShapeDtypeStruct((t, PEER_SLOTS), F32)],
        compiler_params=pltpu.CompilerParams(dimension_semantics=("arbitrary",), vmem_limit_bytes=VMEM_LIMIT),
        name="topk",
    )(sc)


def _gather_rows(row_ref, tab_ref, tile_ref, t):
    for j in range(PEER_SLOTS):
        r = pl.multiple_of(row_ref[t, j], D_CHUNKS)
        tile_ref[pl.ds(j, D_CHUNKS, stride=GATHER_STRIDE), :] = tab_ref[pl.ds(r, D_CHUNKS), :]


def _chunk_operand(tile_ref, c):
    return pltpu.bitcast(tile_ref[pl.ds(c * GATHER_STRIDE, PEER_SLOTS), :], BF16)


def _gather_group(row_ref, tab_ref, tiles, g, n_groups):
    t0 = jnp.minimum(g, n_groups - 1) * PEER_GROUP
    for s in range(PEER_GROUP):
        _gather_rows(row_ref, tab_ref, tiles[s], t0 + s)


def _pipelined_groups(row_ref, tab_ref, tiles, n_groups, compute):
    _gather_group(row_ref, tab_ref, tiles, 0, n_groups)

    def body(g, carry):
        compute(g)
        _gather_group(row_ref, tab_ref, tiles, g + 1, n_groups)
        return carry

    lax.fori_loop(0, n_groups, body, 0)


def _peer_u_kernel(row_ref, h2_ref, g0_ref, g1_ref, tab_ref, coef_ref, *scratch):
    tiles, acc_ref = scratch[:PEER_GROUP], scratch[PEER_GROUP]
    tt = h2_ref.shape[0]
    sub = lax.broadcasted_iota(I32, (PACK_ROWS, 2 * PEER_SLOTS), 0)
    acc_ref[...] = jnp.zeros(acc_ref.shape, F32)

    def compute(g):
        t0 = g * PEER_GROUP
        base = pl.multiple_of((t0 // PACK_ROWS) * PACK_ROWS, PACK_ROWS)
        r0 = t0 - base
        xg = h2_ref[pl.ds(base, PACK_ROWS), :]
        out = acc_ref[pl.ds(base, PACK_ROWS), :]
        for s in range(PEER_GROUP):
            res = _dot_t(xg[:, 0:LANES], _chunk_operand(tiles[s], 0))
            for c in range(1, D_CHUNKS):
                res = res + _dot_t(xg[:, c * LANES:(c + 1) * LANES], _chunk_operand(tiles[s], c))
            out = jnp.where(sub == r0 + s, res, out)
        acc_ref[pl.ds(base, PACK_ROWS), :] = out

    _pipelined_groups(row_ref, tab_ref, tiles, tt // PEER_GROUP, compute)
    src = lax.broadcasted_iota(I32, (PEER_SLOTS, 2 * PEER_SLOTS), 0)
    dst = lax.broadcasted_iota(I32, (PEER_SLOTS, 2 * PEER_SLOTS), 1)
    e0 = jnp.where(dst == 2 * src, 1.0, 0.0).astype(BF16)
    e1 = jnp.where(dst == 2 * src + 1, 1.0, 0.0).astype(BF16)
    w2 = _dot(g0_ref[...].astype(BF16), e0) + _dot(g1_ref[...].astype(BF16), e1)
    coef_ref[...] = (_gelu(acc_ref[...]) * w2).astype(BF16)


def _tile_scratch():
    return [pltpu.VMEM((D_CHUNKS * GATHER_STRIDE, LANES), I32) for _ in range(PEER_GROUP)]


def _peer_u_call(rows, h2, g0, g1, u_tab, tt):
    t = h2.shape[0]
    tok = lambda i: (i, 0)
    return pl.pallas_call(
        _peer_u_kernel,
        grid=(t // tt,),
        in_specs=[
            pl.BlockSpec((tt, PEER_SLOTS), tok, memory_space=pltpu.SMEM),
            pl.BlockSpec((tt, D_MODEL), tok),
            pl.BlockSpec((tt, PEER_SLOTS), tok),
            pl.BlockSpec((tt, PEER_SLOTS), tok),
            pl.BlockSpec(u_tab.shape, lambda i: (0, 0), pipeline_mode=pl.Buffered(1)),
        ],
        out_specs=pl.BlockSpec((tt, 2 * PEER_SLOTS), tok),
        out_shape=jax.ShapeDtypeStruct((t, 2 * PEER_SLOTS), BF16),
        scratch_shapes=_tile_scratch() + [pltpu.VMEM((tt, 2 * PEER_SLOTS), F32)],
        compiler_params=pltpu.CompilerParams(dimension_semantics=("arbitrary",),
                                             vmem_limit_bytes=VMEM_LIMIT_TABLE),
        name="peer_u",
    )(rows, h2, g0, g1, u_tab)


def _peer_v_kernel(row_ref, coef_ref, x1_ref, gf_ref, tab_ref, out_ref, *scratch):
    tiles, acc_ref = scratch[:PEER_GROUP], scratch[PEER_GROUP]
    tt = coef_ref.shape[0]
    sub = lax.broadcasted_iota(I32, (PACK_ROWS, LANES), 0)
    acc_ref[...] = jnp.zeros(acc_ref.shape, F32)

    def compute(g):
        t0 = g * PEER_GROUP
        base = pl.multiple_of((t0 // PACK_ROWS) * PACK_ROWS, PACK_ROWS)
        r0 = t0 - base
        cg = coef_ref[pl.ds(base, PACK_ROWS), :]
        outs = [acc_ref[pl.ds(base, PACK_ROWS), c * LANES:(c + 1) * LANES] for c in range(D_CHUNKS)]
        for s in range(PEER_GROUP):
            for c in range(D_CHUNKS):
                outs[c] = jnp.where(sub == r0 + s, _dot(cg, _chunk_operand(tiles[s], c)), outs[c])
        for c in range(D_CHUNKS):
            acc_ref[pl.ds(base, PACK_ROWS), c * LANES:(c + 1) * LANES] = outs[c]

    _pipelined_groups(row_ref, tab_ref, tiles, tt // PEER_GROUP, compute)
    out_ref[...] = _rms(x1_ref[...] + acc_ref[...], gf_ref[...])


def _peer_v_call(rows, coef, x1, gf, v_tab, tt):
    t = x1.shape[0]
    tok = lambda i: (i, 0)
    return pl.pallas_call(
        _peer_v_kernel,
        grid=(t // tt,),
        in_specs=[
            pl.BlockSpec((tt, PEER_SLOTS), tok, memory_space=pltpu.SMEM),
            pl.BlockSpec((tt, 2 * PEER_SLOTS), tok),
            pl.BlockSpec((tt, D_MODEL), tok),
            pl.BlockSpec((1, D_MODEL), lambda i: (0, 0)),
            pl.BlockSpec(v_tab.shape, lambda i: (0, 0), pipeline_mode=pl.Buffered(1)),
        ],
        out_specs=pl.BlockSpec((tt, D_MODEL), tok),
        out_shape=jax.ShapeDtypeStruct((t, D_MODEL), F32),
        scratch_shapes=_tile_scratch() + [pltpu.VMEM((tt, D_MODEL), F32)],
        compiler_params=pltpu.CompilerParams(dimension_semantics=("arbitrary",),
                                             vmem_limit_bytes=VMEM_LIMIT_TABLE),
        name="peer_v",
    )(rows, coef, x1, gf, v_tab)


def _pack_pairs(tab):
    bits = lax.bitcast_convert_type(tab.astype(BF16), jnp.uint16).astype(jnp.uint32)
    word = bits[:PEER_PAIR_ROWS] | (bits[PEER_PAIR_ROWS:] << 16)
    return lax.bitcast_convert_type(word, I32).reshape(PEER_PAIR_ROWS * D_CHUNKS, LANES)


def _rope_tables(seq):
    half = DA_HEAD_DIM // 2
    inv = 1.0 / (ROPE_THETA ** (jnp.arange(0, DA_HEAD_DIM, 2, dtype=F32) / DA_HEAD_DIM))
    ang = jnp.arange(seq, dtype=F32)[:, None] * inv[None, :]
    cos, sin = jnp.cos(ang), jnp.sin(ang)
    reps = LANES // DA_HEAD_DIM
    cos_t = jnp.tile(jnp.concatenate([cos, cos], axis=1), (1, reps))
    sin_t = jnp.tile(jnp.concatenate([-sin, sin], axis=1), (1, reps))
    del half
    return cos_t, sin_t


def kernel(x, norm1_g, w_in, lambda_q1, lambda_k1, lambda_q2, lambda_k2, subln_g, sgu_ln_g, sgu_ln_b, w_spatial, b_spatial, w_branch_a, w_branch_b, w_gate, b_gate, w_out, norm2_g, peer_wq, peer_subkeys, peer_u, peer_v, norm_f_g):
    b, s, d = x.shape
    t = b * s
    l = 0
    x2 = x.reshape(t, d)
    cos_t, sin_t = _rope_tables(s)
    w_cat = jnp.concatenate([w_in[l], w_gate[l]], axis=1).astype(BF16)

    tk = min(ATTN_TK, s)
    q, k0, k1, vt, u_act, sv_ln, gates = _proj_call(
        x2, norm1_g[l][None], w_cat, b_gate[l][None], cos_t, sin_t, sgu_ln_g[l][None], sgu_ln_b[l][None],
        seq=s, tm=tk)

    lam4 = jnp.stack([lambda_q1[l], lambda_k1[l], lambda_q2[l], lambda_k2[l]]).astype(F32)
    ya = _attn_call(lam4, q, k0, k1, vt, subln_g[l][None], batch=b, tq=min(ATTN_TQ, s), tk=tk)

    b_s_full = jnp.repeat(b_spatial[l].T, SG_GROUP_DIM, axis=1)
    sk = peer_subkeys[l].reshape(2 * PEER_HEADS, PEER_NKEYS, PEER_HALF).astype(BF16)
    x1, h2, sc = _mix_call(x2, u_act, sv_ln, gates, ya, w_spatial[l], b_s_full,
                           w_branch_a[l].astype(BF16), w_branch_b[l].astype(BF16), w_out[l].astype(BF16),
                           norm2_g[l][None], peer_wq[l].astype(BF16), sk, tm=min(MIX_TM, t))

    rows, g0, g1 = _topk_call(sc, tt=min(TOPK_TT, t))
    coef = _peer_u_call(rows, h2, g0, g1, _pack_pairs(peer_u[l]), tt=min(PEER_TT, t))
    out = _peer_v_call(rows, coef, x1, norm_f_g[None], _pack_pairs(peer_v[l]), tt=min(PEER_TT, t))
    return out.reshape(b, s, d)
```

```python
import functools
import math

import jax
import jax.numpy as jnp
from jax import lax
from jax.experimental import pallas as pl
from jax.experimental.pallas import tpu as pltpu

F32 = jnp.float32
BF16 = jnp.bfloat16
I32 = jnp.int32

D_MODEL = 1024
CHUNK = 64
ROPE_THETA = 10000.0
EPS = 1e-6

DA_HEADS = 4
DA_HEAD_DIM = 64
DA_V_DIM = 2 * DA_HEAD_DIM
DA_WIDTH = DA_HEADS * DA_V_DIM

SG_GROUPS = 4
SG_BLOCK = 128
SG_GROUP_DIM = 128
SG_WIDTH = SG_GROUPS * SG_GROUP_DIM

PEER_HEADS = 8
PEER_NKEYS = 128
PEER_HALF = 128
PEER_TOPK = 16
PEER_SLOTS = PEER_HEADS * PEER_TOPK
PEER_N_EXPERTS = PEER_NKEYS * PEER_NKEYS

LANES = 128
SUBLANES = 8
PACK_ROWS = 16
PEER_GROUP = 8
D_HALF = D_MODEL // 2
ROW_WORDS = D_HALF // LANES
IDX_SLOTS = 4

ATTN_TQ = 512
ATTN_TK = 512
MIX_TM = 256
TOPK_TT = 256
PEER_TT = 512

VMEM_LIMIT = 48 * 1024 * 1024
VMEM_LIMIT_TABLE = 56 * 1024 * 1024

LAMBDA_INIT = 0.8 - 0.6 * math.exp(-0.3 * 0)


def _gelu(x):
    return 0.5 * x * (1.0 + lax.erf(x * (0.5 ** 0.5)))


def _rms(x, g):
    return x * lax.rsqrt(jnp.mean(x * x, axis=-1, keepdims=True) + EPS) * g


def _dot(a, b):
    return jnp.dot(a, b, preferred_element_type=F32)


def _dot_t(a, b):
    return lax.dot_general(a, b, (((1,), (1,)), ((), ())), preferred_element_type=F32)


def _proj_kernel(x_ref, g1_ref, w_ref, bg_ref, cos_ref, sin_ref, lng_ref, lnb_ref,
                 q_ref, k0_ref, k1_ref, vt_ref, u_ref, sv_ref, gate_ref):
    tm = x_ref.shape[0]
    hb = _rms(x_ref[...], g1_ref[...]).astype(BF16)
    cos = cos_ref[...]
    sin = sin_ref[...]
    lane = lax.broadcasted_iota(I32, (tm, LANES), 1)
    first_half = (lane % DA_HEAD_DIM) < (DA_HEAD_DIM // 2)

    def rope(t):
        outs = []
        for b in range(DA_WIDTH // LANES):
            blk = t[:, b * LANES:(b + 1) * LANES]
            swapped = jnp.where(first_half, pltpu.roll(blk, LANES - DA_HEAD_DIM // 2, 1),
                                pltpu.roll(blk, DA_HEAD_DIM // 2, 1))
            outs.append(blk * cos + swapped * sin)
        return jnp.concatenate(outs, axis=1)

    o = 0
    q = _dot(hb, w_ref[:, o:o + DA_WIDTH]); o += DA_WIDTH
    q_ref[...] = (rope(q) * (DA_HEAD_DIM ** -0.5)).astype(BF16)
    k = rope(_dot(hb, w_ref[:, o:o + DA_WIDTH])); o += DA_WIDTH
    comp0 = (lax.broadcasted_iota(I32, k.shape, 1) % DA_V_DIM) < DA_HEAD_DIM
    k0_ref[...] = jnp.where(comp0, k, 0.0).astype(BF16)
    k1_ref[...] = jnp.where(comp0, 0.0, k).astype(BF16)
    vt_ref[0] = _dot(hb, w_ref[:, o:o + DA_WIDTH]).T.astype(BF16); o += DA_WIDTH
    u_ref[...] = _gelu(_dot(hb, w_ref[:, o:o + SG_WIDTH])).astype(BF16); o += SG_WIDTH
    sv = _gelu(_dot(hb, w_ref[:, o:o + SG_WIDTH])); o += SG_WIDTH
    mu = jnp.mean(sv, axis=-1, keepdims=True)
    svc = sv - mu
    sv_ref[...] = (svc * lax.rsqrt(jnp.mean(svc * svc, axis=-1, keepdims=True) + EPS) * lng_ref[...]
                   + lnb_ref[...]).astype(BF16)
    gate_ref[...] = jax.nn.sigmoid(_dot(hb, w_ref[:, o:o + 2 * D_MODEL]) + bg_ref[...]).astype(BF16)


def _proj_call(x2, g1, w_cat, b_gate, cos_t, sin_t, ln_g, ln_b, seq, tm):
    t = x2.shape[0]
    nseq = seq // tm
    row = lambda i: (i, 0)
    fixed = lambda i: (0, 0)
    pos = lambda i: (i % nseq, 0)
    wcols = w_cat.shape[1]
    out_shapes = [jax.ShapeDtypeStruct((t, DA_WIDTH), BF16)] * 3 + \
                 [jax.ShapeDtypeStruct((t // tm, DA_WIDTH, tm), BF16)] + \
                 [jax.ShapeDtypeStruct((t, SG_WIDTH), BF16)] * 2 + \
                 [jax.ShapeDtypeStruct((t, 2 * D_MODEL), BF16)]
    return pl.pallas_call(
        _proj_kernel,
        grid=(t // tm,),
        in_specs=[
            pl.BlockSpec((tm, D_MODEL), row),
            pl.BlockSpec((1, D_MODEL), fixed),
            pl.BlockSpec((D_MODEL, wcols), fixed),
            pl.BlockSpec((1, 2 * D_MODEL), fixed),
            pl.BlockSpec((tm, LANES), pos),
            pl.BlockSpec((tm, LANES), pos),
            pl.BlockSpec((1, SG_WIDTH), fixed),
            pl.BlockSpec((1, SG_WIDTH), fixed),
        ],
        out_specs=[pl.BlockSpec((tm, DA_WIDTH), row)] * 3 +
                  [pl.BlockSpec((1, DA_WIDTH, tm), lambda i: (i, 0, 0))] +
                  [pl.BlockSpec((tm, SG_WIDTH), row)] * 2 +
                  [pl.BlockSpec((tm, 2 * D_MODEL), row)],
        out_shape=out_shapes,
        compiler_params=pltpu.CompilerParams(dimension_semantics=("arbitrary",), vmem_limit_bytes=VMEM_LIMIT),
        name="proj",
    )(x2, g1, w_cat, b_gate, cos_t, sin_t, ln_g, ln_b)


def _attn_kernel(lam_ref, q_ref, k0_ref, k1_ref, vt_ref, sg_ref, o_ref, acc_ref, m_ref, l_ref, *, tq, tk):
    i = pl.program_id(2)
    q = q_ref[...]

    m_ref[...] = jnp.full(m_ref.shape, -1e30, F32)
    l_ref[...] = jnp.zeros(l_ref.shape, F32)
    acc_ref[...] = jnp.zeros(acc_ref.shape, F32)

    def step(j, masked):
        start = pl.multiple_of(j * tk, tk)
        vt = vt_ref[j]
        if masked:
            keyc = (j * tk + lax.broadcasted_iota(I32, (tk, tq), 0)) // CHUNK
            qryc = (i * tq + lax.broadcasted_iota(I32, (tk, tq), 1)) // CHUNK
            visible = keyc <= qryc
        for c, k_ref in enumerate((k0_ref, k1_ref)):
            st = _dot_t(k_ref[pl.ds(start, tk), :], q)
            if masked:
                st = jnp.where(visible, st, -jnp.inf)
            m_old = m_ref[c]
            m_new = jnp.maximum(m_old, jnp.max(st, axis=0, keepdims=True))
            alpha = jnp.exp(m_old - m_new)
            p = jnp.exp(st - m_new)
            l_ref[c] = alpha * l_ref[c] + jnp.sum(p, axis=0, keepdims=True)
            acc_ref[c] = alpha * acc_ref[c] + _dot(vt, p.astype(BF16))
            m_ref[c] = m_new

    nfull = i * (tq // tk)

    def body(j, carry):
        step(j, False)
        return carry

    lax.fori_loop(0, nfull, body, 0)
    for jj in range(tq // tk):
        step(nfull + jj, True)

    lam = (jnp.exp(jnp.sum(lam_ref[0:1, :] * lam_ref[1:2, :], axis=-1, keepdims=True))
           - jnp.exp(jnp.sum(lam_ref[2:3, :] * lam_ref[3:4, :], axis=-1, keepdims=True)) + LAMBDA_INIT)
    o = acc_ref[0] / l_ref[0] - lam * (acc_ref[1] / l_ref[1])
    o = o * lax.rsqrt(jnp.mean(o * o, axis=0, keepdims=True) + EPS)
    o_ref[...] = (o.T * sg_ref[...] * (1.0 - LAMBDA_INIT)).astype(BF16)


def _attn_call(lam4, q, k0, k1, vt, subln_g, batch, tq, tk):
    t = q.shape[0]
    s = t // batch
    nq, nk = s // tq, s // tk
    kern = functools.partial(_attn_kernel, tq=tq, tk=tk)
    qblk = pl.BlockSpec((tq, DA_V_DIM), lambda bi, h, i: (bi * nq + i, h))
    kblk = pl.BlockSpec((s, DA_V_DIM), lambda bi, h, i: (bi, h))
    return pl.pallas_call(
        kern,
        grid=(batch, DA_HEADS, nq),
        in_specs=[
            pl.BlockSpec((4, DA_HEAD_DIM), lambda bi, h, i: (0, 0)),
            qblk, kblk, kblk,
            pl.BlockSpec((nk, DA_V_DIM, tk), lambda bi, h, i: (bi, h, 0)),
            pl.BlockSpec((1, DA_V_DIM), lambda bi, h, i: (0, 0)),
        ],
        out_specs=qblk,
        out_shape=jax.ShapeDtypeStruct((t, DA_WIDTH), BF16),
        scratch_shapes=[
            pltpu.VMEM((2, DA_V_DIM, tq), F32),
            pltpu.VMEM((2, 1, tq), F32),
            pltpu.VMEM((2, 1, tq), F32),
        ],
        compiler_params=pltpu.CompilerParams(dimension_semantics=("arbitrary",) * 3, vmem_limit_bytes=VMEM_LIMIT),
        name="attn",
    )(lam4, q, k0, k1, vt, subln_g)


def _mix_kernel(x_ref, u_ref, sv_ref, gate_ref, ya_ref, ws_ref, bs_ref, wa_ref, wb_ref, wo_ref, g2_ref,
                wq_ref, sk_ref, x1_ref, h2_ref, sc_ref):
    tm = x_ref.shape[0]
    pc = lax.broadcasted_iota(I32, (SG_BLOCK, SG_BLOCK), 0) // CHUNK
    pr = lax.broadcasted_iota(I32, (SG_BLOCK, SG_BLOCK), 1) // CHUNK
    causal = pr <= pc
    rows = []
    for n in range(tm // SG_BLOCK):
        cols = []
        for g in range(SG_GROUPS):
            w = jnp.where(causal, ws_ref[g], 0.0).astype(BF16)
            vb = sv_ref[n * SG_BLOCK:(n + 1) * SG_BLOCK, g * SG_GROUP_DIM:(g + 1) * SG_GROUP_DIM]
            cols.append(_dot(w, vb))
        rows.append(jnp.concatenate(cols, axis=1) + bs_ref[...])
    s = jnp.concatenate(rows, axis=0)
    yb = (u_ref[...].astype(F32) * s).astype(BF16)
    merged = (gate_ref[:, :D_MODEL].astype(F32) * _dot(ya_ref[...], wa_ref[...])
              + gate_ref[:, D_MODEL:].astype(F32) * _dot(yb, wb_ref[...]))
    x1 = x_ref[...] + _dot(merged.astype(BF16), wo_ref[...])
    x1_ref[...] = x1
    h2 = _rms(x1, g2_ref[...]).astype(BF16)
    h2_ref[...] = h2
    qp = _dot(h2, wq_ref[...]).astype(BF16)
    for hp in range(2 * PEER_HEADS):
        sc_ref[hp] = _dot_t(sk_ref[hp], qp[:, hp * PEER_HALF:(hp + 1) * PEER_HALF])


def _mix_call(x2, u_act, sv_ln, gates, ya, w_s, b_s_full, w_a, w_b, w_o, g2, w_q, sk, tm):
    t = x2.shape[0]
    row = lambda i: (i, 0)
    fixed2 = lambda i: (0, 0)
    fixed3 = lambda i: (0, 0, 0)
    return pl.pallas_call(
        _mix_kernel,
        grid=(t // tm,),
        in_specs=[
            pl.BlockSpec((tm, D_MODEL), row),
            pl.BlockSpec((tm, SG_WIDTH), row),
            pl.BlockSpec((tm, SG_WIDTH), row),
            pl.BlockSpec((tm, 2 * D_MODEL), row),
            pl.BlockSpec((tm, DA_WIDTH), row),
            pl.BlockSpec((SG_GROUPS, SG_BLOCK, SG_BLOCK), fixed3),
            pl.BlockSpec((SG_BLOCK, SG_WIDTH), fixed2),
            pl.BlockSpec((DA_WIDTH, D_MODEL), fixed2),
            pl.BlockSpec((SG_WIDTH, D_MODEL), fixed2),
            pl.BlockSpec((D_MODEL, D_MODEL), fixed2),
            pl.BlockSpec((1, D_MODEL), fixed2),
            pl.BlockSpec((D_MODEL, 2 * PEER_HEADS * PEER_HALF), fixed2),
            pl.BlockSpec((2 * PEER_HEADS, PEER_NKEYS, PEER_HALF), fixed3),
        ],
        out_specs=[
            pl.BlockSpec((tm, D_MODEL), row),
            pl.BlockSpec((tm, D_MODEL), row),
            pl.BlockSpec((2 * PEER_HEADS, PEER_NKEYS, tm), lambda i: (0, 0, i)),
        ],
        out_shape=[
            jax.ShapeDtypeStruct((t, D_MODEL), F32),
            jax.ShapeDtypeStruct((t, D_MODEL), BF16),
            jax.ShapeDtypeStruct((2 * PEER_HEADS, PEER_NKEYS, t), F32),
        ],
        compiler_params=pltpu.CompilerParams(dimension_semantics=("arbitrary",), vmem_limit_bytes=VMEM_LIMIT),
        name="mix",
    )(x2, u_act, sv_ln, gates, ya, w_s, b_s_full, w_a, w_b, w_o, g2, w_q, sk)


def _top16(vals, n):
    tt = vals.shape[1]
    pos = lax.broadcasted_iota(I32, (n, tt), 0)
    cur = vals
    out_v, out_i = [], []
    for _ in range(PEER_TOPK):
        m = jnp.max(cur, axis=0, keepdims=True)
        idx = jnp.min(jnp.where(cur == m, pos, n), axis=0, keepdims=True)
        out_v.append(m)
        out_i.append(idx)
        cur = jnp.where(pos == idx, -jnp.inf, cur)
    return out_v, out_i


def _topk_kernel(sc_ref, row_ref, g_ref):
    tt = sc_ref.shape[2]
    k = PEER_TOPK
    rows_all, g_all = [], []
    for h in range(PEER_HEADS):
        s1, i1 = _top16(sc_ref[2 * h], PEER_NKEYS)
        s2, i2 = _top16(sc_ref[2 * h + 1], PEER_NKEYS)
        s2m = jnp.concatenate(s2, axis=0)
        i2m = jnp.concatenate(i2, axis=0)
        keep = [k // (a + 1) for a in range(k)]
        n_cand = -(-sum(keep) // SUBLANES) * SUBLANES
        pad = n_cand - sum(keep)
        cand = jnp.concatenate([s1[a] + s2m[:keep[a]] for a in range(k)]
                               + [jnp.full((pad, tt), -jnp.inf, F32)], axis=0)
        cidx = jnp.concatenate([i1[a] * PEER_NKEYS + i2m[:keep[a]] for a in range(k)]
                               + [jnp.zeros((pad, tt), I32)], axis=0)
        top_s, pos = _top16(cand, n_cand)
        cpos = lax.broadcasted_iota(I32, (n_cand, tt), 0)
        e = [jnp.sum(jnp.where(cpos == pos[r], cidx, 0), axis=0, keepdims=True) for r in range(k)]
        w = [jnp.exp(top_s[r] - top_s[0]) for r in range(k)]
        den = w[0]
        for r in range(1, k):
            den = den + w[r]
        rows_all.append(jnp.concatenate(e, axis=0) * ROW_WORDS)
        g_all.append(jnp.concatenate(w, axis=0) / den)
    row_ref[...] = jnp.concatenate(rows_all, axis=0).T
    g_ref[...] = jnp.concatenate(g_all, axis=0).T


def _topk_call(sc, tt):
    t = sc.shape[2]
    blk = pl.BlockSpec((tt, PEER_SLOTS), lambda i: (i, 0))
    return pl.pallas_call(
        _topk_kernel,
        grid=(t // tt,),
        in_specs=[pl.BlockSpec((2 * PEER_HEADS, PEER_NKEYS, tt), lambda i: (0, 0, i))],
        out_specs=[blk, blk],
        out_shape=[jax.ShapeDtypeStruct((t, PEER_SLOTS), I32),
                   jax.ShapeDtypeStruct((t, PEER_SLOTS), F32)],
        compiler_params=pltpu.CompilerParams(dimension_semantics=("arbitrary",), vmem_limit_bytes=VMEM_LIMIT),
        name="topk",
    )(sc)


def _chunk_operand(tile_ref, c):
    return pltpu.bitcast(tile_ref[pl.ds(c, PEER_SLOTS, stride=ROW_WORDS), :], BF16)


def _gather_group(idx_ref, slot, tab_ref, tiles):
    for s in range(PEER_GROUP):
        for j in range(PEER_SLOTS):
            r = pl.multiple_of(idx_ref[slot, s, j], ROW_WORDS)
            tiles[s][pl.ds(j * ROW_WORDS, ROW_WORDS), :] = tab_ref[pl.ds(r, ROW_WORDS), :]


def _pipelined_groups(rows_hbm, idx_ref, sem, tab_ref, tiles, n_groups, compute):
    first = pl.program_id(0) * n_groups

    def idx_copy(g, slot):
        src = rows_hbm.at[first + jnp.minimum(g, n_groups - 1)]
        return pltpu.make_async_copy(src, idx_ref.at[slot], sem.at[slot])

    for g in range(IDX_SLOTS - 1):
        idx_copy(g, g).start()
    idx_copy(0, 0).wait()
    _gather_group(idx_ref, 0, tab_ref, tiles)
    idx_copy(IDX_SLOTS - 1, IDX_SLOTS - 1).start()

    def body(it, carry):
        for k in range(IDX_SLOTS):
            g = IDX_SLOTS * it + k
            nxt = (k + 1) % IDX_SLOTS
            idx_copy(g + 1, nxt).wait()
            compute(g)
            _gather_group(idx_ref, nxt, tab_ref, tiles)
            idx_copy(g + IDX_SLOTS, k).start()
        return carry

    lax.fori_loop(0, n_groups // IDX_SLOTS, body, 0)
    for slot in range(1, IDX_SLOTS):
        idx_copy(n_groups, slot).wait()


def _group_rows(g):
    t0 = g * PEER_GROUP
    base = pl.multiple_of((t0 // PACK_ROWS) * PACK_ROWS, PACK_ROWS)
    return base, t0 - base


def _peer_u_kernel(rows_hbm, h2_ref, g_ref, tab_ref, coef_ref, idx_ref, sem, *scratch):
    tiles, acc_lo, acc_hi = scratch[:PEER_GROUP], scratch[PEER_GROUP], scratch[PEER_GROUP + 1]
    tt = h2_ref.shape[0]
    width = 2 * PEER_SLOTS
    sub = lax.broadcasted_iota(I32, (PACK_ROWS, width), 0)
    acc_lo[...] = jnp.zeros(acc_lo.shape, F32)
    acc_hi[...] = jnp.zeros(acc_hi.shape, F32)

    def compute(g):
        base, r0 = _group_rows(g)
        xg = h2_ref[pl.ds(base, PACK_ROWS), :]
        lhs = [jnp.concatenate([xg[:, c * LANES:(c + 1) * LANES],
                                xg[:, D_HALF + c * LANES:D_HALF + (c + 1) * LANES]], axis=0)
               for c in range(ROW_WORDS)]
        lo = acc_lo[pl.ds(base, PACK_ROWS), :]
        hi = acc_hi[pl.ds(base, PACK_ROWS), :]
        for s in range(PEER_GROUP):
            res = _dot_t(lhs[0], _chunk_operand(tiles[s], 0))
            for c in range(1, ROW_WORDS):
                res = res + _dot_t(lhs[c], _chunk_operand(tiles[s], c))
            lo = jnp.where(sub == r0 + s, res[:PACK_ROWS], lo)
            hi = jnp.where(sub == r0 + s, res[PACK_ROWS:], hi)
        acc_lo[pl.ds(base, PACK_ROWS), :] = lo
        acc_hi[pl.ds(base, PACK_ROWS), :] = hi

    _pipelined_groups(rows_hbm, idx_ref, sem, tab_ref, tiles, tt // PEER_GROUP, compute)
    col = lax.broadcasted_iota(I32, (tt, width), 1)
    halves = jnp.where(col % 2 == 0, acc_lo[...], acc_hi[...])
    act = _gelu(halves + pltpu.roll(halves, width - 1, 1))
    src = lax.broadcasted_iota(I32, (PEER_SLOTS, width), 0)
    dst = lax.broadcasted_iota(I32, (PEER_SLOTS, width), 1)
    spread = jnp.where(dst == 2 * src, 1.0, 0.0).astype(BF16)
    coef = act * _dot(g_ref[...].astype(BF16), spread)
    coef_ref[:, :width] = coef.astype(BF16)
    coef_ref[:, width:] = pltpu.roll(coef, 1, 1).astype(BF16)


def _gather_scratch():
    return ([pltpu.SMEM((IDX_SLOTS, PEER_GROUP, PEER_SLOTS), I32), pltpu.SemaphoreType.DMA((IDX_SLOTS,))]
            + [pltpu.VMEM((ROW_WORDS * PEER_SLOTS, LANES), I32) for _ in range(PEER_GROUP)])


def _peer_u_call(rows, h2, g, u_tab, tt):
    t = h2.shape[0]
    tok = lambda i: (i, 0)
    width = 2 * PEER_SLOTS
    return pl.pallas_call(
        _peer_u_kernel,
        grid=(t // tt,),
        in_specs=[
            pl.BlockSpec(memory_space=pl.ANY),
            pl.BlockSpec((tt, D_MODEL), tok),
            pl.BlockSpec((tt, PEER_SLOTS), tok),
            pl.BlockSpec(u_tab.shape, lambda i: (0, 0), pipeline_mode=pl.Buffered(1)),
        ],
        out_specs=pl.BlockSpec((tt, 2 * width), tok),
        out_shape=jax.ShapeDtypeStruct((t, 2 * width), BF16),
        scratch_shapes=_gather_scratch() + [pltpu.VMEM((tt, width), F32)] * 2,
        compiler_params=pltpu.CompilerParams(dimension_semantics=("arbitrary",),
                                             vmem_limit_bytes=VMEM_LIMIT_TABLE),
        name="peer_u",
    )(rows, h2, g, u_tab)


def _peer_v_kernel(rows_hbm, coef_ref, x1_ref, gf_ref, tab_ref, out_ref, idx_ref, sem, *scratch):
    tiles, acc_ref = scratch[:PEER_GROUP], scratch[PEER_GROUP]
    tt = coef_ref.shape[0]
    width = 2 * PEER_SLOTS
    sub = lax.broadcasted_iota(I32, (PACK_ROWS, LANES), 0)
    acc_ref[...] = jnp.zeros(acc_ref.shape, F32)

    def compute(g):
        base, r0 = _group_rows(g)
        cg = jnp.concatenate([coef_ref[pl.ds(base, PACK_ROWS), :width],
                              coef_ref[pl.ds(base, PACK_ROWS), width:]], axis=0)
        cols = [h * D_HALF + c * LANES for c in range(ROW_WORDS) for h in range(2)]
        outs = [acc_ref[pl.ds(base, PACK_ROWS), o:o + LANES] for o in cols]
        for s in range(PEER_GROUP):
            for c in range(ROW_WORDS):
                res = _dot(cg, _chunk_operand(tiles[s], c))
                outs[2 * c] = jnp.where(sub == r0 + s, res[:PACK_ROWS], outs[2 * c])
                outs[2 * c + 1] = jnp.where(sub == r0 + s, res[PACK_ROWS:], outs[2 * c + 1])
        for o, val in zip(cols, outs):
            acc_ref[pl.ds(base, PACK_ROWS), o:o + LANES] = val

    _pipelined_groups(rows_hbm, idx_ref, sem, tab_ref, tiles, tt // PEER_GROUP, compute)
    out_ref[...] = _rms(x1_ref[...] + acc_ref[...], gf_ref[...])


def _peer_v_call(rows, coef, x1, gf, v_tab, tt):
    t = x1.shape[0]
    tok = lambda i: (i, 0)
    return pl.pallas_call(
        _peer_v_kernel,
        grid=(t // tt,),
        in_specs=[
            pl.BlockSpec(memory_space=pl.ANY),
            pl.BlockSpec((tt, 4 * PEER_SLOTS), tok),
            pl.BlockSpec((tt, D_MODEL), tok),
            pl.BlockSpec((1, D_MODEL), lambda i: (0, 0)),
            pl.BlockSpec(v_tab.shape, lambda i: (0, 0), pipeline_mode=pl.Buffered(1)),
        ],
        out_specs=pl.BlockSpec((tt, D_MODEL), tok),
        out_shape=jax.ShapeDtypeStruct((t, D_MODEL), F32),
        scratch_shapes=_gather_scratch() + [pltpu.VMEM((tt, D_MODEL), F32)],
        compiler_params=pltpu.CompilerParams(dimension_semantics=("arbitrary",),
                                             vmem_limit_bytes=VMEM_LIMIT_TABLE),
        name="peer_v",
    )(rows, coef, x1, gf, v_tab)


def _pack_rows(tab):
    bits = lax.bitcast_convert_type(tab.astype(BF16), jnp.uint16).astype(jnp.uint32)
    word = bits[:, :D_HALF] | (bits[:, D_HALF:] << 16)
    return lax.bitcast_convert_type(word, I32).reshape(tab.shape[0] * ROW_WORDS, LANES)


def _rope_tables(seq):
    half = DA_HEAD_DIM // 2
    inv = 1.0 / (ROPE_THETA ** (jnp.arange(0, DA_HEAD_DIM, 2, dtype=F32) / DA_HEAD_DIM))
    ang = jnp.arange(seq, dtype=F32)[:, None] * inv[None, :]
    cos, sin = jnp.cos(ang), jnp.sin(ang)
    reps = LANES // DA_HEAD_DIM
    cos_t = jnp.tile(jnp.concatenate([cos, cos], axis=1), (1, reps))
    sin_t = jnp.tile(jnp.concatenate([-sin, sin], axis=1), (1, reps))
    del half
    return cos_t, sin_t


def kernel(x, norm1_g, w_in, lambda_q1, lambda_k1, lambda_q2, lambda_k2, subln_g, sgu_ln_g, sgu_ln_b, w_spatial, b_spatial, w_branch_a, w_branch_b, w_gate, b_gate, w_out, norm2_g, peer_wq, peer_subkeys, peer_u, peer_v, norm_f_g):
    b, s, d = x.shape
    t = b * s
    l = 0
    x2 = x.reshape(t, d)
    cos_t, sin_t = _rope_tables(s)
    w_cat = jnp.concatenate([w_in[l], w_gate[l]], axis=1).astype(BF16)

    tk = min(ATTN_TK, s)
    q, k0, k1, vt, u_act, sv_ln, gates = _proj_call(
        x2, norm1_g[l][None], w_cat, b_gate[l][None], cos_t, sin_t, sgu_ln_g[l][None], sgu_ln_b[l][None],
        seq=s, tm=tk)

    lam4 = jnp.stack([lambda_q1[l], lambda_k1[l], lambda_q2[l], lambda_k2[l]]).astype(F32)
    ya = _attn_call(lam4, q, k0, k1, vt, subln_g[l][None], batch=b, tq=min(ATTN_TQ, s), tk=tk)

    b_s_full = jnp.repeat(b_spatial[l].T, SG_GROUP_DIM, axis=1)
    sk = peer_subkeys[l].reshape(2 * PEER_HEADS, PEER_NKEYS, PEER_HALF).astype(BF16)
    x1, h2, sc = _mix_call(x2, u_act, sv_ln, gates, ya, w_spatial[l], b_s_full,
                           w_branch_a[l].astype(BF16), w_branch_b[l].astype(BF16), w_out[l].astype(BF16),
                           norm2_g[l][None], peer_wq[l].astype(BF16), sk, tm=min(MIX_TM, t))

    rows, g = _topk_call(sc, tt=min(TOPK_TT, t))
    rows = rows.reshape(t // PEER_GROUP, PEER_GROUP, PEER_SLOTS)
    coef = _peer_u_call(rows, h2, g, _pack_rows(peer_u[l]), tt=min(PEER_TT, t))
    out = _peer_v_call(rows, coef, x1, norm_f_g[None], _pack_rows(peer_v[l]), tt=min(PEER_TT, t))
    return out.reshape(b, s, d)
```

```python
import functools
import math

import jax
import jax.numpy as jnp
from jax import lax
from jax.experimental import pallas as pl
from jax.experimental.pallas import tpu as pltpu

F32 = jnp.float32
BF16 = jnp.bfloat16
I32 = jnp.int32

D_MODEL = 1024
CHUNK = 64
ROPE_THETA = 10000.0
EPS = 1e-6

DA_HEADS = 4
DA_HEAD_DIM = 64
DA_V_DIM = 2 * DA_HEAD_DIM
DA_WIDTH = DA_HEADS * DA_V_DIM

SG_GROUPS = 4
SG_BLOCK = 128
SG_GROUP_DIM = 128
SG_WIDTH = SG_GROUPS * SG_GROUP_DIM

PEER_HEADS = 8
PEER_NKEYS = 128
PEER_HALF = 128
PEER_TOPK = 16
PEER_SLOTS = PEER_HEADS * PEER_TOPK
PEER_N_EXPERTS = PEER_NKEYS * PEER_NKEYS

LANES = 128
SUBLANES = 8
PACK_ROWS = 16
PEER_GROUP = 8
D_HALF = D_MODEL // 2
ROW_WORDS = D_HALF // LANES
IDX_SLOTS = 8

ATTN_TQ = 512
ATTN_TK = 512
MIX_TM = 256
TOPK_TT = 256
PEER_TT = 512

VMEM_LIMIT = 48 * 1024 * 1024
VMEM_LIMIT_TABLE = 56 * 1024 * 1024

LAMBDA_INIT = 0.8 - 0.6 * math.exp(-0.3 * 0)


def _gelu(x):
    return 0.5 * x * (1.0 + lax.erf(x * (0.5 ** 0.5)))


def _rms(x, g):
    return x * lax.rsqrt(jnp.mean(x * x, axis=-1, keepdims=True) + EPS) * g


def _dot(a, b):
    return jnp.dot(a, b, preferred_element_type=F32)


def _dot_t(a, b):
    return lax.dot_general(a, b, (((1,), (1,)), ((), ())), preferred_element_type=F32)


def _proj_kernel(x_ref, g1_ref, w_ref, bg_ref, cos_ref, sin_ref, lng_ref, lnb_ref,
                 q_ref, k0_ref, k1_ref, vt_ref, u_ref, sv_ref, gate_ref):
    tm = x_ref.shape[0]
    hb = _rms(x_ref[...], g1_ref[...]).astype(BF16)
    cos = cos_ref[...]
    sin = sin_ref[...]
    lane = lax.broadcasted_iota(I32, (tm, LANES), 1)
    first_half = (lane % DA_HEAD_DIM) < (DA_HEAD_DIM // 2)

    def rope(t):
        outs = []
        for b in range(DA_WIDTH // LANES):
            blk = t[:, b * LANES:(b + 1) * LANES]
            swapped = jnp.where(first_half, pltpu.roll(blk, LANES - DA_HEAD_DIM // 2, 1),
                                pltpu.roll(blk, DA_HEAD_DIM // 2, 1))
            outs.append(blk * cos + swapped * sin)
        return jnp.concatenate(outs, axis=1)

    o = 0
    q = _dot(hb, w_ref[:, o:o + DA_WIDTH]); o += DA_WIDTH
    q_ref[...] = (rope(q) * (DA_HEAD_DIM ** -0.5)).astype(BF16)
    k = rope(_dot(hb, w_ref[:, o:o + DA_WIDTH])); o += DA_WIDTH
    comp0 = (lax.broadcasted_iota(I32, k.shape, 1) % DA_V_DIM) < DA_HEAD_DIM
    k0_ref[...] = jnp.where(comp0, k, 0.0).astype(BF16)
    k1_ref[...] = jnp.where(comp0, 0.0, k).astype(BF16)
    vt_ref[0] = _dot(hb, w_ref[:, o:o + DA_WIDTH]).T.astype(BF16); o += DA_WIDTH
    u_ref[...] = _gelu(_dot(hb, w_ref[:, o:o + SG_WIDTH])).astype(BF16); o += SG_WIDTH
    sv = _gelu(_dot(hb, w_ref[:, o:o + SG_WIDTH])); o += SG_WIDTH
    mu = jnp.mean(sv, axis=-1, keepdims=True)
    svc = sv - mu
    sv_ref[...] = (svc * lax.rsqrt(jnp.mean(svc * svc, axis=-1, keepdims=True) + EPS) * lng_ref[...]
                   + lnb_ref[...]).astype(BF16)
    gate_ref[...] = jax.nn.sigmoid(_dot(hb, w_ref[:, o:o + 2 * D_MODEL]) + bg_ref[...]).astype(BF16)


def _proj_call(x2, g1, w_cat, b_gate, cos_t, sin_t, ln_g, ln_b, seq, tm):
    t = x2.shape[0]
    nseq = seq // tm
    row = lambda i: (i, 0)
    fixed = lambda i: (0, 0)
    pos = lambda i: (i % nseq, 0)
    wcols = w_cat.shape[1]
    out_shapes = [jax.ShapeDtypeStruct((t, DA_WIDTH), BF16)] * 3 + \
                 [jax.ShapeDtypeStruct((t // tm, DA_WIDTH, tm), BF16)] + \
                 [jax.ShapeDtypeStruct((t, SG_WIDTH), BF16)] * 2 + \
                 [jax.ShapeDtypeStruct((t, 2 * D_MODEL), BF16)]
    return pl.pallas_call(
        _proj_kernel,
        grid=(t // tm,),
        in_specs=[
            pl.BlockSpec((tm, D_MODEL), row),
            pl.BlockSpec((1, D_MODEL), fixed),
            pl.BlockSpec((D_MODEL, wcols), fixed),
            pl.BlockSpec((1, 2 * D_MODEL), fixed),
            pl.BlockSpec((tm, LANES), pos),
            pl.BlockSpec((tm, LANES), pos),
            pl.BlockSpec((1, SG_WIDTH), fixed),
            pl.BlockSpec((1, SG_WIDTH), fixed),
        ],
        out_specs=[pl.BlockSpec((tm, DA_WIDTH), row)] * 3 +
                  [pl.BlockSpec((1, DA_WIDTH, tm), lambda i: (i, 0, 0))] +
                  [pl.BlockSpec((tm, SG_WIDTH), row)] * 2 +
                  [pl.BlockSpec((tm, 2 * D_MODEL), row)],
        out_shape=out_shapes,
        compiler_params=pltpu.CompilerParams(dimension_semantics=("arbitrary",), vmem_limit_bytes=VMEM_LIMIT),
        name="proj",
    )(x2, g1, w_cat, b_gate, cos_t, sin_t, ln_g, ln_b)


def _attn_kernel(lam_ref, q_ref, k0_ref, k1_ref, vt_ref, sg_ref, o_ref, acc_ref, m_ref, l_ref, *, tq, tk):
    i = pl.program_id(2)
    q = q_ref[...]

    m_ref[...] = jnp.full(m_ref.shape, -1e30, F32)
    l_ref[...] = jnp.zeros(l_ref.shape, F32)
    acc_ref[...] = jnp.zeros(acc_ref.shape, F32)

    def step(j, masked):
        start = pl.multiple_of(j * tk, tk)
        vt = vt_ref[j]
        if masked:
            keyc = (j * tk + lax.broadcasted_iota(I32, (tk, tq), 0)) // CHUNK
            qryc = (i * tq + lax.broadcasted_iota(I32, (tk, tq), 1)) // CHUNK
            visible = keyc <= qryc
        for c, k_ref in enumerate((k0_ref, k1_ref)):
            st = _dot_t(k_ref[pl.ds(start, tk), :], q)
            if masked:
                st = jnp.where(visible, st, -jnp.inf)
            m_old = m_ref[c]
            m_new = jnp.maximum(m_old, jnp.max(st, axis=0, keepdims=True))
            alpha = jnp.exp(m_old - m_new)
            p = jnp.exp(st - m_new)
            l_ref[c] = alpha * l_ref[c] + jnp.sum(p, axis=0, keepdims=True)
            acc_ref[c] = alpha * acc_ref[c] + _dot(vt, p.astype(BF16))
            m_ref[c] = m_new

    nfull = i * (tq // tk)

    def body(j, carry):
        step(j, False)
        return carry

    lax.fori_loop(0, nfull, body, 0)
    for jj in range(tq // tk):
        step(nfull + jj, True)

    lam = (jnp.exp(jnp.sum(lam_ref[0:1, :] * lam_ref[1:2, :], axis=-1, keepdims=True))
           - jnp.exp(jnp.sum(lam_ref[2:3, :] * lam_ref[3:4, :], axis=-1, keepdims=True)) + LAMBDA_INIT)
    o = acc_ref[0] / l_ref[0] - lam * (acc_ref[1] / l_ref[1])
    o = o * lax.rsqrt(jnp.mean(o * o, axis=0, keepdims=True) + EPS)
    o_ref[...] = (o.T * sg_ref[...] * (1.0 - LAMBDA_INIT)).astype(BF16)


def _attn_call(lam4, q, k0, k1, vt, subln_g, batch, tq, tk):
    t = q.shape[0]
    s = t // batch
    nq, nk = s // tq, s // tk
    kern = functools.partial(_attn_kernel, tq=tq, tk=tk)
    qblk = pl.BlockSpec((tq, DA_V_DIM), lambda bi, h, i: (bi * nq + i, h))
    kblk = pl.BlockSpec((s, DA_V_DIM), lambda bi, h, i: (bi, h))
    return pl.pallas_call(
        kern,
        grid=(batch, DA_HEADS, nq),
        in_specs=[
            pl.BlockSpec((4, DA_HEAD_DIM), lambda bi, h, i: (0, 0)),
            qblk, kblk, kblk,
            pl.BlockSpec((nk, DA_V_DIM, tk), lambda bi, h, i: (bi, h, 0)),
            pl.BlockSpec((1, DA_V_DIM), lambda bi, h, i: (0, 0)),
        ],
        out_specs=qblk,
        out_shape=jax.ShapeDtypeStruct((t, DA_WIDTH), BF16),
        scratch_shapes=[
            pltpu.VMEM((2, DA_V_DIM, tq), F32),
            pltpu.VMEM((2, 1, tq), F32),
            pltpu.VMEM((2, 1, tq), F32),
        ],
        compiler_params=pltpu.CompilerParams(dimension_semantics=("arbitrary",) * 3, vmem_limit_bytes=VMEM_LIMIT),
        name="attn",
    )(lam4, q, k0, k1, vt, subln_g)


def _mix_kernel(x_ref, u_ref, sv_ref, gate_ref, ya_ref, ws_ref, bs_ref, wa_ref, wb_ref, wo_ref, g2_ref,
                wq_ref, sk_ref, x1_ref, h2_ref, sc_ref):
    tm = x_ref.shape[0]
    pc = lax.broadcasted_iota(I32, (SG_BLOCK, SG_BLOCK), 0) // CHUNK
    pr = lax.broadcasted_iota(I32, (SG_BLOCK, SG_BLOCK), 1) // CHUNK
    causal = pr <= pc
    rows = []
    for n in range(tm // SG_BLOCK):
        cols = []
        for g in range(SG_GROUPS):
            w = jnp.where(causal, ws_ref[g], 0.0).astype(BF16)
            vb = sv_ref[n * SG_BLOCK:(n + 1) * SG_BLOCK, g * SG_GROUP_DIM:(g + 1) * SG_GROUP_DIM]
            cols.append(_dot(w, vb))
        rows.append(jnp.concatenate(cols, axis=1) + bs_ref[...])
    s = jnp.concatenate(rows, axis=0)
    yb = (u_ref[...].astype(F32) * s).astype(BF16)
    merged = (gate_ref[:, :D_MODEL].astype(F32) * _dot(ya_ref[...], wa_ref[...])
              + gate_ref[:, D_MODEL:].astype(F32) * _dot(yb, wb_ref[...]))
    x1 = x_ref[...] + _dot(merged.astype(BF16), wo_ref[...])
    x1_ref[...] = x1
    h2 = _rms(x1, g2_ref[...]).astype(BF16)
    h2_ref[...] = h2
    qp = _dot(h2, wq_ref[...]).astype(BF16)
    for hp in range(2 * PEER_HEADS):
        sc_ref[hp] = _dot_t(sk_ref[hp], qp[:, hp * PEER_HALF:(hp + 1) * PEER_HALF])


def _mix_call(x2, u_act, sv_ln, gates, ya, w_s, b_s_full, w_a, w_b, w_o, g2, w_q, sk, tm):
    t = x2.shape[0]
    row = lambda i: (i, 0)
    fixed2 = lambda i: (0, 0)
    fixed3 = lambda i: (0, 0, 0)
    return pl.pallas_call(
        _mix_kernel,
        grid=(t // tm,),
        in_specs=[
            pl.BlockSpec((tm, D_MODEL), row),
            pl.BlockSpec((tm, SG_WIDTH), row),
            pl.BlockSpec((tm, SG_WIDTH), row),
            pl.BlockSpec((tm, 2 * D_MODEL), row),
            pl.BlockSpec((tm, DA_WIDTH), row),
            pl.BlockSpec((SG_GROUPS, SG_BLOCK, SG_BLOCK), fixed3),
            pl.BlockSpec((SG_BLOCK, SG_WIDTH), fixed2),
            pl.BlockSpec((DA_WIDTH, D_MODEL), fixed2),
            pl.BlockSpec((SG_WIDTH, D_MODEL), fixed2),
            pl.BlockSpec((D_MODEL, D_MODEL), fixed2),
            pl.BlockSpec((1, D_MODEL), fixed2),
            pl.BlockSpec((D_MODEL, 2 * PEER_HEADS * PEER_HALF), fixed2),
            pl.BlockSpec((2 * PEER_HEADS, PEER_NKEYS, PEER_HALF), fixed3),
        ],
        out_specs=[
            pl.BlockSpec((tm, D_MODEL), row),
            pl.BlockSpec((tm, D_MODEL), row),
            pl.BlockSpec((2 * PEER_HEADS, PEER_NKEYS, tm), lambda i: (0, 0, i)),
        ],
        out_shape=[
            jax.ShapeDtypeStruct((t, D_MODEL), F32),
            jax.ShapeDtypeStruct((t, D_MODEL), BF16),
            jax.ShapeDtypeStruct((2 * PEER_HEADS, PEER_NKEYS, t), F32),
        ],
        compiler_params=pltpu.CompilerParams(dimension_semantics=("arbitrary",), vmem_limit_bytes=VMEM_LIMIT),
        name="mix",
    )(x2, u_act, sv_ln, gates, ya, w_s, b_s_full, w_a, w_b, w_o, g2, w_q, sk)


def _top16(vals, n):
    tt = vals.shape[1]
    pos = lax.broadcasted_iota(I32, (n, tt), 0)
    cur = vals
    out_v, out_i = [], []
    for _ in range(PEER_TOPK):
        m = jnp.max(cur, axis=0, keepdims=True)
        idx = jnp.min(jnp.where(cur == m, pos, n), axis=0, keepdims=True)
        out_v.append(m)
        out_i.append(idx)
        cur = jnp.where(pos == idx, -jnp.inf, cur)
    return out_v, out_i


def _topk_kernel(sc_ref, row_ref, g_ref):
    tt = sc_ref.shape[2]
    k = PEER_TOPK
    rows_all, g_all = [], []
    for h in range(PEER_HEADS):
        s1, i1 = _top16(sc_ref[2 * h], PEER_NKEYS)
        s2, i2 = _top16(sc_ref[2 * h + 1], PEER_NKEYS)
        s2m = jnp.concatenate(s2, axis=0)
        i2m = jnp.concatenate(i2, axis=0)
        keep = [k // (a + 1) for a in range(k)]
        n_cand = -(-sum(keep) // SUBLANES) * SUBLANES
        pad = n_cand - sum(keep)
        cand = jnp.concatenate([s1[a] + s2m[:keep[a]] for a in range(k)]
                               + [jnp.full((pad, tt), -jnp.inf, F32)], axis=0)
        cidx = jnp.concatenate([i1[a] * PEER_NKEYS + i2m[:keep[a]] for a in range(k)]
                               + [jnp.zeros((pad, tt), I32)], axis=0)
        top_s, pos = _top16(cand, n_cand)
        cpos = lax.broadcasted_iota(I32, (n_cand, tt), 0)
        e = [jnp.sum(jnp.where(cpos == pos[r], cidx, 0), axis=0, keepdims=True) for r in range(k)]
        w = [jnp.exp(top_s[r] - top_s[0]) for r in range(k)]
        den = w[0]
        for r in range(1, k):
            den = den + w[r]
        rows_all.append(jnp.concatenate(e, axis=0) * ROW_WORDS)
        g_all.append(jnp.concatenate(w, axis=0) / den)
    row_ref[...] = jnp.concatenate(rows_all, axis=0).T
    g_ref[...] = jnp.concatenate(g_all, axis=0).T


def _topk_call(sc, tt):
    t = sc.shape[2]
    blk = pl.BlockSpec((tt, PEER_SLOTS), lambda i: (i, 0))
    return pl.pallas_call(
        _topk_kernel,
        grid=(t // tt,),
        in_specs=[pl.BlockSpec((2 * PEER_HEADS, PEER_NKEYS, tt), lambda i: (0, 0, i))],
        out_specs=[blk, blk],
        out_shape=[jax.ShapeDtypeStruct((t, PEER_SLOTS), I32),
                   jax.ShapeDtypeStruct((t, PEER_SLOTS), F32)],
        compiler_params=pltpu.CompilerParams(dimension_semantics=("arbitrary",), vmem_limit_bytes=VMEM_LIMIT),
        name="topk",
    )(sc)


def _chunk_operand(tile_ref, c):
    return pltpu.bitcast(tile_ref[pl.ds(c, PEER_SLOTS, stride=ROW_WORDS), :], BF16)


def _gather_group(idx_ref, slot, tab_ref, tiles):
    for s in range(PEER_GROUP):
        for j in range(PEER_SLOTS):
            r = pl.multiple_of(idx_ref[slot, s, j], ROW_WORDS)
            tiles[s][pl.ds(j * ROW_WORDS, ROW_WORDS), :] = tab_ref[pl.ds(r, ROW_WORDS), :]


def _pipelined_groups(rows_hbm, idx_ref, sem, tab_ref, tiles, n_groups, compute):
    first = pl.program_id(0) * n_groups

    def idx_copy(g, slot):
        src = rows_hbm.at[first + jnp.minimum(g, n_groups - 1)]
        return pltpu.make_async_copy(src, idx_ref.at[slot], sem.at[slot])

    for g in range(IDX_SLOTS - 1):
        idx_copy(g, g).start()
    idx_copy(0, 0).wait()
    _gather_group(idx_ref, 0, tab_ref, tiles)
    idx_copy(IDX_SLOTS - 1, IDX_SLOTS - 1).start()

    def body(it, carry):
        for k in range(IDX_SLOTS):
            g = IDX_SLOTS * it + k
            nxt = (k + 1) % IDX_SLOTS
            idx_copy(g + 1, nxt).wait()
            compute(g)
            _gather_group(idx_ref, nxt, tab_ref, tiles)
            idx_copy(g + IDX_SLOTS, k).start()
        return carry

    lax.fori_loop(0, n_groups // IDX_SLOTS, body, 0)
    for slot in range(1, IDX_SLOTS):
        idx_copy(n_groups, slot).wait()


def _group_rows(g):
    t0 = g * PEER_GROUP
    base = pl.multiple_of((t0 // PACK_ROWS) * PACK_ROWS, PACK_ROWS)
    return base, t0 - base


def _peer_u_kernel(rows_hbm, h2_ref, g_ref, tab_ref, coef_ref, idx_ref, sem, *scratch):
    tiles, acc_lo, acc_hi = scratch[:PEER_GROUP], scratch[PEER_GROUP], scratch[PEER_GROUP + 1]
    tt = h2_ref.shape[0]
    width = 2 * PEER_SLOTS
    sub = lax.broadcasted_iota(I32, (PACK_ROWS, width), 0)
    acc_lo[...] = jnp.zeros(acc_lo.shape, F32)
    acc_hi[...] = jnp.zeros(acc_hi.shape, F32)

    def compute(g):
        base, r0 = _group_rows(g)
        xg = h2_ref[pl.ds(base, PACK_ROWS), :]
        lhs = [jnp.concatenate([xg[:, c * LANES:(c + 1) * LANES],
                                xg[:, D_HALF + c * LANES:D_HALF + (c + 1) * LANES]], axis=0)
               for c in range(ROW_WORDS)]
        lo = acc_lo[pl.ds(base, PACK_ROWS), :]
        hi = acc_hi[pl.ds(base, PACK_ROWS), :]
        for s in range(PEER_GROUP):
            res = _dot_t(lhs[0], _chunk_operand(tiles[s], 0))
            for c in range(1, ROW_WORDS):
                res = res + _dot_t(lhs[c], _chunk_operand(tiles[s], c))
            lo = jnp.where(sub == r0 + s, res[:PACK_ROWS], lo)
            hi = jnp.where(sub == r0 + s, res[PACK_ROWS:], hi)
        acc_lo[pl.ds(base, PACK_ROWS), :] = lo
        acc_hi[pl.ds(base, PACK_ROWS), :] = hi

    _pipelined_groups(rows_hbm, idx_ref, sem, tab_ref, tiles, tt // PEER_GROUP, compute)
    col = lax.broadcasted_iota(I32, (tt, width), 1)
    halves = jnp.where(col % 2 == 0, acc_lo[...], acc_hi[...])
    act = _gelu(halves + pltpu.roll(halves, width - 1, 1))
    src = lax.broadcasted_iota(I32, (PEER_SLOTS, width), 0)
    dst = lax.broadcasted_iota(I32, (PEER_SLOTS, width), 1)
    spread = jnp.where(dst == 2 * src, 1.0, 0.0).astype(BF16)
    coef = act * _dot(g_ref[...].astype(BF16), spread)
    coef_ref[:, :width] = coef.astype(BF16)
    coef_ref[:, width:] = pltpu.roll(coef, 1, 1).astype(BF16)


def _gather_scratch():
    return ([pltpu.SMEM((IDX_SLOTS, PEER_GROUP, PEER_SLOTS), I32), pltpu.SemaphoreType.DMA((IDX_SLOTS,))]
            + [pltpu.VMEM((ROW_WORDS * PEER_SLOTS, LANES), I32) for _ in range(PEER_GROUP)])


def _peer_u_call(rows, h2, g, u_tab, tt):
    t = h2.shape[0]
    tok = lambda i: (i, 0)
    width = 2 * PEER_SLOTS
    return pl.pallas_call(
        _peer_u_kernel,
        grid=(t // tt,),
        in_specs=[
            pl.BlockSpec(memory_space=pl.ANY),
            pl.BlockSpec((tt, D_MODEL), tok),
            pl.BlockSpec((tt, PEER_SLOTS), tok),
            pl.BlockSpec(u_tab.shape, lambda i: (0, 0), pipeline_mode=pl.Buffered(1)),
        ],
        out_specs=pl.BlockSpec((tt, 2 * width), tok),
        out_shape=jax.ShapeDtypeStruct((t, 2 * width), BF16),
        scratch_shapes=_gather_scratch() + [pltpu.VMEM((tt, width), F32)] * 2,
        compiler_params=pltpu.CompilerParams(dimension_semantics=("arbitrary",),
                                             vmem_limit_bytes=VMEM_LIMIT_TABLE),
        name="peer_u",
    )(rows, h2, g, u_tab)


def _peer_v_kernel(rows_hbm, coef_ref, x1_ref, gf_ref, tab_ref, out_ref, idx_ref, sem, *scratch):
    tiles, acc_ref = scratch[:PEER_GROUP], scratch[PEER_GROUP]
    tt = coef_ref.shape[0]
    width = 2 * PEER_SLOTS
    sub = lax.broadcasted_iota(I32, (PACK_ROWS, LANES), 0)
    acc_ref[...] = jnp.zeros(acc_ref.shape, F32)

    def compute(g):
        base, r0 = _group_rows(g)
        cg = jnp.concatenate([coef_ref[pl.ds(base, PACK_ROWS), :width],
                              coef_ref[pl.ds(base, PACK_ROWS), width:]], axis=0)
        cols = [h * D_HALF + c * LANES for c in range(ROW_WORDS) for h in range(2)]
        outs = [acc_ref[pl.ds(base, PACK_ROWS), o:o + LANES] for o in cols]
        for s in range(PEER_GROUP):
            for c in range(ROW_WORDS):
                res = _dot(cg, _chunk_operand(tiles[s], c))
                outs[2 * c] = jnp.where(sub == r0 + s, res[:PACK_ROWS], outs[2 * c])
                outs[2 * c + 1] = jnp.where(sub == r0 + s, res[PACK_ROWS:], outs[2 * c + 1])
        for o, val in zip(cols, outs):
            acc_ref[pl.ds(base, PACK_ROWS), o:o + LANES] = val

    _pipelined_groups(rows_hbm, idx_ref, sem, tab_ref, tiles, tt // PEER_GROUP, compute)
    out_ref[...] = _rms(x1_ref[...] + acc_ref[...], gf_ref[...])


def _peer_v_call(rows, coef, x1, gf, v_tab, tt):
    t = x1.shape[0]
    tok = lambda i: (i, 0)
    return pl.pallas_call(
        _peer_v_kernel,
        grid=(t // tt,),
        in_specs=[
            pl.BlockSpec(memory_space=pl.ANY),
            pl.BlockSpec((tt, 4 * PEER_SLOTS), tok),
            pl.BlockSpec((tt, D_MODEL), tok),
            pl.BlockSpec((1, D_MODEL), lambda i: (0, 0)),
            pl.BlockSpec(v_tab.shape, lambda i: (0, 0), pipeline_mode=pl.Buffered(1)),
        ],
        out_specs=pl.BlockSpec((tt, D_MODEL), tok),
        out_shape=jax.ShapeDtypeStruct((t, D_MODEL), F32),
        scratch_shapes=_gather_scratch() + [pltpu.VMEM((tt, D_MODEL), F32)],
        compiler_params=pltpu.CompilerParams(dimension_semantics=("arbitrary",),
                                             vmem_limit_bytes=VMEM_LIMIT_TABLE),
        name="peer_v",
    )(rows, coef, x1, gf, v_tab)


def _pack_rows(tab):
    bits = lax.bitcast_convert_type(tab.astype(BF16), jnp.uint16).astype(jnp.uint32)
    word = bits[:, :D_HALF] | (bits[:, D_HALF:] << 16)
    return lax.bitcast_convert_type(word, I32).reshape(tab.shape[0] * ROW_WORDS, LANES)


def _rope_tables(seq):
    half = DA_HEAD_DIM // 2
    inv = 1.0 / (ROPE_THETA ** (jnp.arange(0, DA_HEAD_DIM, 2, dtype=F32) / DA_HEAD_DIM))
    ang = jnp.arange(seq, dtype=F32)[:, None] * inv[None, :]
    cos, sin = jnp.cos(ang), jnp.sin(ang)
    reps = LANES // DA_HEAD_DIM
    cos_t = jnp.tile(jnp.concatenate([cos, cos], axis=1), (1, reps))
    sin_t = jnp.tile(jnp.concatenate([-sin, sin], axis=1), (1, reps))
    del half
    return cos_t, sin_t


def kernel(x, norm1_g, w_in, lambda_q1, lambda_k1, lambda_q2, lambda_k2, subln_g, sgu_ln_g, sgu_ln_b, w_spatial, b_spatial, w_branch_a, w_branch_b, w_gate, b_gate, w_out, norm2_g, peer_wq, peer_subkeys, peer_u, peer_v, norm_f_g):
    b, s, d = x.shape
    t = b * s
    l = 0
    x2 = x.reshape(t, d)
    cos_t, sin_t = _rope_tables(s)
    w_cat = jnp.concatenate([w_in[l], w_gate[l]], axis=1).astype(BF16)

    tk = min(ATTN_TK, s)
    q, k0, k1, vt, u_act, sv_ln, gates = _proj_call(
        x2, norm1_g[l][None], w_cat, b_gate[l][None], cos_t, sin_t, sgu_ln_g[l][None], sgu_ln_b[l][None],
        seq=s, tm=tk)

    lam4 = jnp.stack([lambda_q1[l], lambda_k1[l], lambda_q2[l], lambda_k2[l]]).astype(F32)
    ya = _attn_call(lam4, q, k0, k1, vt, subln_g[l][None], batch=b, tq=min(ATTN_TQ, s), tk=tk)

    b_s_full = jnp.repeat(b_spatial[l].T, SG_GROUP_DIM, axis=1)
    sk = peer_subkeys[l].reshape(2 * PEER_HEADS, PEER_NKEYS, PEER_HALF).astype(BF16)
    x1, h2, sc = _mix_call(x2, u_act, sv_ln, gates, ya, w_spatial[l], b_s_full,
                           w_branch_a[l].astype(BF16), w_branch_b[l].astype(BF16), w_out[l].astype(BF16),
                           norm2_g[l][None], peer_wq[l].astype(BF16), sk, tm=min(MIX_TM, t))

    rows, g = _topk_call(sc, tt=min(TOPK_TT, t))
    rows = rows.reshape(t // PEER_GROUP, PEER_GROUP, PEER_SLOTS)
    coef = _peer_u_call(rows, h2, g, _pack_rows(peer_u[l]), tt=min(PEER_TT, t))
    out = _peer_v_call(rows, coef, x1, norm_f_g[None], _pack_rows(peer_v[l]), tt=min(PEER_TT, t))
    return out.reshape(b, s, d)
```

```python
import functools
import math

import jax
import jax.numpy as jnp
from jax import lax
from jax.experimental import pallas as pl
from jax.experimental.pallas import tpu as pltpu

F32 = jnp.float32
BF16 = jnp.bfloat16
I32 = jnp.int32

D_MODEL = 1024
CHUNK = 64
ROPE_THETA = 10000.0
EPS = 1e-6

DA_HEADS = 4
DA_HEAD_DIM = 64
DA_V_DIM = 2 * DA_HEAD_DIM
DA_WIDTH = DA_HEADS * DA_V_DIM

SG_GROUPS = 4
SG_BLOCK = 128
SG_GROUP_DIM = 128
SG_WIDTH = SG_GROUPS * SG_GROUP_DIM

PEER_HEADS = 8
PEER_NKEYS = 128
PEER_HALF = 128
PEER_TOPK = 16
PEER_SLOTS = PEER_HEADS * PEER_TOPK
PEER_N_EXPERTS = PEER_NKEYS * PEER_NKEYS

LANES = 128
SUBLANES = 8
PACK_ROWS = 16
PEER_GROUP = 8
D_HALF = D_MODEL // 2
ROW_WORDS = D_HALF // LANES
IDX_SLOTS = 4

ATTN_TQ = 512
MIX_TM = 256
TOPK_TT = 256
PEER_TT = 512

VMEM_LIMIT = 48 * 1024 * 1024
VMEM_LIMIT_TABLE = 56 * 1024 * 1024

LAMBDA_INIT = 0.8 - 0.6 * math.exp(-0.3 * 0)


def _gelu(x):
    return 0.5 * x * (1.0 + lax.erf(x * (0.5 ** 0.5)))


def _rms(x, g):
    return x * lax.rsqrt(jnp.mean(x * x, axis=-1, keepdims=True) + EPS) * g


def _dot(a, b):
    return jnp.dot(a, b, preferred_element_type=F32)


def _dot_t(a, b):
    return lax.dot_general(a, b, (((1,), (1,)), ((), ())), preferred_element_type=F32)


def _proj_kernel(x_ref, g1_ref, w_ref, bg_ref, cos_ref, sin_ref, lng_ref, lnb_ref,
                 q_ref, k0_ref, k1_ref, vt_ref, u_ref, sv_ref, gate_ref):
    tm = x_ref.shape[0]
    hb = _rms(x_ref[...], g1_ref[...]).astype(BF16)
    cos = cos_ref[...]
    sin = sin_ref[...]
    lane = lax.broadcasted_iota(I32, (tm, LANES), 1)
    first_half = (lane % DA_HEAD_DIM) < (DA_HEAD_DIM // 2)

    def rope(t):
        outs = []
        for b in range(DA_WIDTH // LANES):
            blk = t[:, b * LANES:(b + 1) * LANES]
            swapped = jnp.where(first_half, pltpu.roll(blk, LANES - DA_HEAD_DIM // 2, 1),
                                pltpu.roll(blk, DA_HEAD_DIM // 2, 1))
            outs.append(blk * cos + swapped * sin)
        return jnp.concatenate(outs, axis=1)

    o = 0
    q = _dot(hb, w_ref[:, o:o + DA_WIDTH]); o += DA_WIDTH
    q_ref[...] = (rope(q) * (DA_HEAD_DIM ** -0.5)).astype(BF16)
    k = rope(_dot(hb, w_ref[:, o:o + DA_WIDTH])); o += DA_WIDTH
    comp0 = (lax.broadcasted_iota(I32, k.shape, 1) % DA_V_DIM) < DA_HEAD_DIM
    k0_ref[...] = jnp.where(comp0, k, 0.0).astype(BF16)
    k1_ref[...] = jnp.where(comp0, 0.0, k).astype(BF16)
    vt_ref[0] = _dot(hb, w_ref[:, o:o + DA_WIDTH]).T.astype(BF16); o += DA_WIDTH
    u_ref[...] = _gelu(_dot(hb, w_ref[:, o:o + SG_WIDTH])).astype(BF16); o += SG_WIDTH
    sv = _gelu(_dot(hb, w_ref[:, o:o + SG_WIDTH])); o += SG_WIDTH
    mu = jnp.mean(sv, axis=-1, keepdims=True)
    svc = sv - mu
    sv_ref[...] = (svc * lax.rsqrt(jnp.mean(svc * svc, axis=-1, keepdims=True) + EPS) * lng_ref[...]
                   + lnb_ref[...]).astype(BF16)
    gate_ref[...] = jax.nn.sigmoid(_dot(hb, w_ref[:, o:o + 2 * D_MODEL]) + bg_ref[...]).astype(BF16)


def _proj_call(x2, g1, w_cat, b_gate, cos_t, sin_t, ln_g, ln_b, seq, tm):
    t = x2.shape[0]
    nseq = seq // tm
    row = lambda i: (i, 0)
    fixed = lambda i: (0, 0)
    pos = lambda i: (i % nseq, 0)
    wcols = w_cat.shape[1]
    out_shapes = [jax.ShapeDtypeStruct((t, DA_WIDTH), BF16)] * 3 + \
                 [jax.ShapeDtypeStruct((t // tm, DA_WIDTH, tm), BF16)] + \
                 [jax.ShapeDtypeStruct((t, SG_WIDTH), BF16)] * 2 + \
                 [jax.ShapeDtypeStruct((t, 2 * D_MODEL), BF16)]
    return pl.pallas_call(
        _proj_kernel,
        grid=(t // tm,),
        in_specs=[
            pl.BlockSpec((tm, D_MODEL), row),
            pl.BlockSpec((1, D_MODEL), fixed),
            pl.BlockSpec((D_MODEL, wcols), fixed),
            pl.BlockSpec((1, 2 * D_MODEL), fixed),
            pl.BlockSpec((tm, LANES), pos),
            pl.BlockSpec((tm, LANES), pos),
            pl.BlockSpec((1, SG_WIDTH), fixed),
            pl.BlockSpec((1, SG_WIDTH), fixed),
        ],
        out_specs=[pl.BlockSpec((tm, DA_WIDTH), row)] * 3 +
                  [pl.BlockSpec((1, DA_WIDTH, tm), lambda i: (i, 0, 0))] +
                  [pl.BlockSpec((tm, SG_WIDTH), row)] * 2 +
                  [pl.BlockSpec((tm, 2 * D_MODEL), row)],
        out_shape=out_shapes,
        compiler_params=pltpu.CompilerParams(dimension_semantics=("arbitrary",), vmem_limit_bytes=VMEM_LIMIT),
        name="proj",
    )(x2, g1, w_cat, b_gate, cos_t, sin_t, ln_g, ln_b)


def _attn_kernel(lam_ref, q_ref, k0_ref, k1_ref, vt_ref, sg_ref, o_ref, acc_ref, st_ref, m_ref, mprev_ref, l_ref,
                 *, tq):
    i = pl.program_id(2)
    q = q_ref[...]

    m_ref[...] = jnp.full(m_ref.shape, -1e30, F32)
    l_ref[...] = jnp.zeros(l_ref.shape, F32)
    acc_ref[...] = jnp.zeros(acc_ref.shape, F32)

    def scores(j, masked):
        start = pl.multiple_of(j * tq, tq)
        if masked:
            keyc = lax.broadcasted_iota(I32, (tq, tq), 0) // CHUNK
            qryc = lax.broadcasted_iota(I32, (tq, tq), 1) // CHUNK
            visible = keyc <= qryc
        for c, k_ref in enumerate((k0_ref, k1_ref)):
            st = _dot_t(k_ref[pl.ds(start, tq), :], q)
            if masked:
                st = jnp.where(visible, st, -jnp.inf)
            st_ref[c] = st
            m_old = m_ref[c]
            mprev_ref[c] = m_old
            m_ref[c] = jnp.maximum(m_old, jnp.max(st, axis=0, keepdims=True))

    def accumulate(j):
        vt = vt_ref[j]
        for c in range(2):
            m_new = m_ref[c]
            alpha = jnp.exp(mprev_ref[c] - m_new)
            p = jnp.exp(st_ref[c] - m_new)
            l_ref[c] = alpha * l_ref[c] + jnp.sum(p, axis=0, keepdims=True)
            acc_ref[c] = alpha * acc_ref[c] + _dot(vt, p.astype(BF16))

    @pl.when(i == 0)
    def _():
        scores(0, True)

    @pl.when(i > 0)
    def _():
        scores(0, False)

    def body(j, carry):
        accumulate(j)
        scores(j + 1, False)
        return carry

    lax.fori_loop(0, i - 1, body, 0)

    @pl.when(i > 0)
    def _():
        accumulate(i - 1)
        scores(i, True)

    accumulate(i)

    lam = (jnp.exp(jnp.sum(lam_ref[0:1, :] * lam_ref[1:2, :], axis=-1, keepdims=True))
           - jnp.exp(jnp.sum(lam_ref[2:3, :] * lam_ref[3:4, :], axis=-1, keepdims=True)) + LAMBDA_INIT)
    o = acc_ref[0] / l_ref[0] - lam * (acc_ref[1] / l_ref[1])
    o = o * lax.rsqrt(jnp.mean(o * o, axis=0, keepdims=True) + EPS)
    o_ref[...] = (o.T * sg_ref[...] * (1.0 - LAMBDA_INIT)).astype(BF16)


def _attn_call(lam4, q, k0, k1, vt, subln_g, batch, tq):
    t = q.shape[0]
    s = t // batch
    nq = s // tq
    kern = functools.partial(_attn_kernel, tq=tq)
    qblk = pl.BlockSpec((tq, DA_V_DIM), lambda bi, h, i: (bi * nq + i, h))
    kblk = pl.BlockSpec((s, DA_V_DIM), lambda bi, h, i: (bi, h))
    stat = pltpu.VMEM((2, 1, tq), F32)
    return pl.pallas_call(
        kern,
        grid=(batch, DA_HEADS, nq),
        in_specs=[
            pl.BlockSpec((4, DA_HEAD_DIM), lambda bi, h, i: (0, 0)),
            qblk, kblk, kblk,
            pl.BlockSpec((nq, DA_V_DIM, tq), lambda bi, h, i: (bi, h, 0)),
            pl.BlockSpec((1, DA_V_DIM), lambda bi, h, i: (0, 0)),
        ],
        out_specs=qblk,
        out_shape=jax.ShapeDtypeStruct((t, DA_WIDTH), BF16),
        scratch_shapes=[
            pltpu.VMEM((2, DA_V_DIM, tq), F32),
            pltpu.VMEM((2, tq, tq), F32),
            stat, stat, stat,
        ],
        compiler_params=pltpu.CompilerParams(dimension_semantics=("arbitrary",) * 3, vmem_limit_bytes=VMEM_LIMIT),
        name="attn",
    )(lam4, q, k0, k1, vt, subln_g)


def _mix_kernel(x_ref, u_ref, sv_ref, gate_ref, ya_ref, ws_ref, bs_ref, wa_ref, wb_ref, wo_ref, g2_ref,
                wq_ref, sk_ref, x1_ref, h2_ref, sc_ref):
    tm = x_ref.shape[0]
    pc = lax.broadcasted_iota(I32, (SG_BLOCK, SG_BLOCK), 0) // CHUNK
    pr = lax.broadcasted_iota(I32, (SG_BLOCK, SG_BLOCK), 1) // CHUNK
    causal = pr <= pc
    rows = []
    for n in range(tm // SG_BLOCK):
        cols = []
        for g in range(SG_GROUPS):
            w = jnp.where(causal, ws_ref[g], 0.0).astype(BF16)
            vb = sv_ref[n * SG_BLOCK:(n + 1) * SG_BLOCK, g * SG_GROUP_DIM:(g + 1) * SG_GROUP_DIM]
            cols.append(_dot(w, vb))
        rows.append(jnp.concatenate(cols, axis=1) + bs_ref[...])
    s = jnp.concatenate(rows, axis=0)
    yb = (u_ref[...].astype(F32) * s).astype(BF16)
    merged = (gate_ref[:, :D_MODEL].astype(F32) * _dot(ya_ref[...], wa_ref[...])
              + gate_ref[:, D_MODEL:].astype(F32) * _dot(yb, wb_ref[...]))
    x1 = x_ref[...] + _dot(merged.astype(BF16), wo_ref[...])
    x1_ref[...] = x1
    h2 = _rms(x1, g2_ref[...]).astype(BF16)
    h2_ref[...] = h2
    qp = _dot(h2, wq_ref[...]).astype(BF16)
    for hp in range(2 * PEER_HEADS):
        sc_ref[hp] = _dot_t(sk_ref[hp], qp[:, hp * PEER_HALF:(hp + 1) * PEER_HALF])


def _mix_call(x2, u_act, sv_ln, gates, ya, w_s, b_s_full, w_a, w_b, w_o, g2, w_q, sk, tm):
    t = x2.shape[0]
    row = lambda i: (i, 0)
    fixed2 = lambda i: (0, 0)
    fixed3 = lambda i: (0, 0, 0)
    return pl.pallas_call(
        _mix_kernel,
        grid=(t // tm,),
        in_specs=[
            pl.BlockSpec((tm, D_MODEL), row),
            pl.BlockSpec((tm, SG_WIDTH), row),
            pl.BlockSpec((tm, SG_WIDTH), row),
            pl.BlockSpec((tm, 2 * D_MODEL), row),
            pl.BlockSpec((tm, DA_WIDTH), row),
            pl.BlockSpec((SG_GROUPS, SG_BLOCK, SG_BLOCK), fixed3),
            pl.BlockSpec((SG_BLOCK, SG_WIDTH), fixed2),
            pl.BlockSpec((DA_WIDTH, D_MODEL), fixed2),
            pl.BlockSpec((SG_WIDTH, D_MODEL), fixed2),
            pl.BlockSpec((D_MODEL, D_MODEL), fixed2),
            pl.BlockSpec((1, D_MODEL), fixed2),
            pl.BlockSpec((D_MODEL, 2 * PEER_HEADS * PEER_HALF), fixed2),
            pl.BlockSpec((2 * PEER_HEADS, PEER_NKEYS, PEER_HALF), fixed3),
        ],
        out_specs=[
            pl.BlockSpec((tm, D_MODEL), row),
            pl.BlockSpec((tm, D_MODEL), row),
            pl.BlockSpec((2 * PEER_HEADS, PEER_NKEYS, tm), lambda i: (0, 0, i)),
        ],
        out_shape=[
            jax.ShapeDtypeStruct((t, D_MODEL), F32),
            jax.ShapeDtypeStruct((t, D_MODEL), BF16),
            jax.ShapeDtypeStruct((2 * PEER_HEADS, PEER_NKEYS, t), F32),
        ],
        compiler_params=pltpu.CompilerParams(dimension_semantics=("arbitrary",), vmem_limit_bytes=VMEM_LIMIT),
        name="mix",
    )(x2, u_act, sv_ln, gates, ya, w_s, b_s_full, w_a, w_b, w_o, g2, w_q, sk)


def _top16(vals, n):
    tt = vals.shape[1]
    pos = lax.broadcasted_iota(I32, (n, tt), 0)
    cur = vals
    out_v, out_i = [], []
    for _ in range(PEER_TOPK):
        m = jnp.max(cur, axis=0, keepdims=True)
        idx = jnp.min(jnp.where(cur == m, pos, n), axis=0, keepdims=True)
        out_v.append(m)
        out_i.append(idx)
        cur = jnp.where(pos == idx, -jnp.inf, cur)
    return out_v, out_i


def _topk_kernel(sc_ref, row_ref, g_ref):
    tt = sc_ref.shape[2]
    k = PEER_TOPK
    rows_all, g_all = [], []
    for h in range(PEER_HEADS):
        s1, i1 = _top16(sc_ref[2 * h], PEER_NKEYS)
        s2, i2 = _top16(sc_ref[2 * h + 1], PEER_NKEYS)
        s2m = jnp.concatenate(s2, axis=0)
        i2m = jnp.concatenate(i2, axis=0)
        keep = [k // (a + 1) for a in range(k)]
        n_cand = -(-sum(keep) // SUBLANES) * SUBLANES
        pad = n_cand - sum(keep)
        cand = jnp.concatenate([s1[a] + s2m[:keep[a]] for a in range(k)]
                               + [jnp.full((pad, tt), -jnp.inf, F32)], axis=0)
        cidx = jnp.concatenate([i1[a] * PEER_NKEYS + i2m[:keep[a]] for a in range(k)]
                               + [jnp.zeros((pad, tt), I32)], axis=0)
        top_s, pos = _top16(cand, n_cand)
        cpos = lax.broadcasted_iota(I32, (n_cand, tt), 0)
        e = [jnp.sum(jnp.where(cpos == pos[r], cidx, 0), axis=0, keepdims=True) for r in range(k)]
        w = [jnp.exp(top_s[r] - top_s[0]) for r in range(k)]
        den = w[0]
        for r in range(1, k):
            den = den + w[r]
        rows_all.append(jnp.concatenate(e, axis=0) * ROW_WORDS)
        g_all.append(jnp.concatenate(w, axis=0) / den)
    row_ref[...] = jnp.concatenate(rows_all, axis=0).T
    g_ref[...] = jnp.concatenate(g_all, axis=0).T


def _topk_call(sc, tt):
    t = sc.shape[2]
    blk = pl.BlockSpec((tt, PEER_SLOTS), lambda i: (i, 0))
    return pl.pallas_call(
        _topk_kernel,
        grid=(t // tt,),
        in_specs=[pl.BlockSpec((2 * PEER_HEADS, PEER_NKEYS, tt), lambda i: (0, 0, i))],
        out_specs=[blk, blk],
        out_shape=[jax.ShapeDtypeStruct((t, PEER_SLOTS), I32),
                   jax.ShapeDtypeStruct((t, PEER_SLOTS), F32)],
        compiler_params=pltpu.CompilerParams(dimension_semantics=("arbitrary",), vmem_limit_bytes=VMEM_LIMIT),
        name="topk",
    )(sc)


def _chunk_operand(tile_ref, c):
    return pltpu.bitcast(tile_ref[pl.ds(c, PEER_SLOTS, stride=ROW_WORDS), :], BF16)


def _gather_group(idx_ref, slot, tab_ref, tiles):
    for s in range(PEER_GROUP):
        for j in range(PEER_SLOTS):
            r = pl.multiple_of(idx_ref[slot, s, j], ROW_WORDS)
            tiles[s][pl.ds(j * ROW_WORDS, ROW_WORDS), :] = tab_ref[pl.ds(r, ROW_WORDS), :]


def _pipelined_groups(rows_hbm, idx_ref, sem, tab_ref, tiles, n_groups, compute):
    first = pl.program_id(0) * n_groups

    def idx_copy(g, slot):
        src = rows_hbm.at[first + jnp.minimum(g, n_groups - 1)]
        return pltpu.make_async_copy(src, idx_ref.at[slot], sem.at[slot])

    for g in range(IDX_SLOTS - 1):
        idx_copy(g, g).start()
    idx_copy(0, 0).wait()
    _gather_group(idx_ref, 0, tab_ref, tiles)
    idx_copy(IDX_SLOTS - 1, IDX_SLOTS - 1).start()

    def body(it, carry):
        for k in range(IDX_SLOTS):
            g = IDX_SLOTS * it + k
            nxt = (k + 1) % IDX_SLOTS
            idx_copy(g + 1, nxt).wait()
            compute(g)
            _gather_group(idx_ref, nxt, tab_ref, tiles)
            idx_copy(g + IDX_SLOTS, k).start()
        return carry

    lax.fori_loop(0, n_groups // IDX_SLOTS, body, 0)
    for slot in range(1, IDX_SLOTS):
        idx_copy(n_groups, slot).wait()


def _group_rows(g):
    t0 = g * PEER_GROUP
    base = pl.multiple_of((t0 // PACK_ROWS) * PACK_ROWS, PACK_ROWS)
    return base, t0 - base


def _peer_u_kernel(rows_hbm, h2_ref, g_ref, tab_ref, coef_ref, idx_ref, sem, *scratch):
    tiles, acc_lo, acc_hi = scratch[:PEER_GROUP], scratch[PEER_GROUP], scratch[PEER_GROUP + 1]
    tt = h2_ref.shape[0]
    width = 2 * PEER_SLOTS
    sub = lax.broadcasted_iota(I32, (PACK_ROWS, width), 0)
    acc_lo[...] = jnp.zeros(acc_lo.shape, F32)
    acc_hi[...] = jnp.zeros(acc_hi.shape, F32)

    def compute(g):
        base, r0 = _group_rows(g)
        xg = h2_ref[pl.ds(base, PACK_ROWS), :]
        lhs = [jnp.concatenate([xg[:, c * LANES:(c + 1) * LANES],
                                xg[:, D_HALF + c * LANES:D_HALF + (c + 1) * LANES]], axis=0)
               for c in range(ROW_WORDS)]
        lo = acc_lo[pl.ds(base, PACK_ROWS), :]
        hi = acc_hi[pl.ds(base, PACK_ROWS), :]
        for s in range(PEER_GROUP):
            res = _dot_t(lhs[0], _chunk_operand(tiles[s], 0))
            for c in range(1, ROW_WORDS):
                res = res + _dot_t(lhs[c], _chunk_operand(tiles[s], c))
            lo = jnp.where(sub == r0 + s, res[:PACK_ROWS], lo)
            hi = jnp.where(sub == r0 + s, res[PACK_ROWS:], hi)
        acc_lo[pl.ds(base, PACK_ROWS), :] = lo
        acc_hi[pl.ds(base, PACK_ROWS), :] = hi

    _pipelined_groups(rows_hbm, idx_ref, sem, tab_ref, tiles, tt // PEER_GROUP, compute)
    col = lax.broadcasted_iota(I32, (tt, width), 1)
    halves = jnp.where(col % 2 == 0, acc_lo[...], acc_hi[...])
    act = _gelu(halves + pltpu.roll(halves, width - 1, 1))
    src = lax.broadcasted_iota(I32, (PEER_SLOTS, width), 0)
    dst = lax.broadcasted_iota(I32, (PEER_SLOTS, width), 1)
    spread = jnp.where(dst == 2 * src, 1.0, 0.0).astype(BF16)
    coef = act * _dot(g_ref[...].astype(BF16), spread)
    coef_ref[:, :width] = coef.astype(BF16)
    coef_ref[:, width:] = pltpu.roll(coef, 1, 1).astype(BF16)


def _gather_scratch():
    return ([pltpu.SMEM((IDX_SLOTS, PEER_GROUP, PEER_SLOTS), I32), pltpu.SemaphoreType.DMA((IDX_SLOTS,))]
            + [pltpu.VMEM((ROW_WORDS * PEER_SLOTS, LANES), I32) for _ in range(PEER_GROUP)])


def _peer_u_call(rows, h2, g, u_tab, tt):
    t = h2.shape[0]
    tok = lambda i: (i, 0)
    width = 2 * PEER_SLOTS
    return pl.pallas_call(
        _peer_u_kernel,
        grid=(t // tt,),
        in_specs=[
            pl.BlockSpec(memory_space=pl.ANY),
            pl.BlockSpec((tt, D_MODEL), tok),
            pl.BlockSpec((tt, PEER_SLOTS), tok),
            pl.BlockSpec(u_tab.shape, lambda i: (0, 0), pipeline_mode=pl.Buffered(1)),
        ],
        out_specs=pl.BlockSpec((tt, 2 * width), tok),
        out_shape=jax.ShapeDtypeStruct((t, 2 * width), BF16),
        scratch_shapes=_gather_scratch() + [pltpu.VMEM((tt, width), F32)] * 2,
        compiler_params=pltpu.CompilerParams(dimension_semantics=("arbitrary",),
                                             vmem_limit_bytes=VMEM_LIMIT_TABLE),
        name="peer_u",
    )(rows, h2, g, u_tab)


def _peer_v_kernel(rows_hbm, coef_ref, x1_ref, gf_ref, tab_ref, out_ref, idx_ref, sem, *scratch):
    tiles, acc_ref = scratch[:PEER_GROUP], scratch[PEER_GROUP]
    tt = coef_ref.shape[0]
    width = 2 * PEER_SLOTS
    sub = lax.broadcasted_iota(I32, (PACK_ROWS, LANES), 0)
    acc_ref[...] = jnp.zeros(acc_ref.shape, F32)

    def compute(g):
        base, r0 = _group_rows(g)
        cg = jnp.concatenate([coef_ref[pl.ds(base, PACK_ROWS), :width],
                              coef_ref[pl.ds(base, PACK_ROWS), width:]], axis=0)
        cols = [h * D_HALF + c * LANES for c in range(ROW_WORDS) for h in range(2)]
        outs = [acc_ref[pl.ds(base, PACK_ROWS), o:o + LANES] for o in cols]
        for s in range(PEER_GROUP):
            for c in range(ROW_WORDS):
                res = _dot(cg, _chunk_operand(tiles[s], c))
                outs[2 * c] = jnp.where(sub == r0 + s, res[:PACK_ROWS], outs[2 * c])
                outs[2 * c + 1] = jnp.where(sub == r0 + s, res[PACK_ROWS:], outs[2 * c + 1])
        for o, val in zip(cols, outs):
            acc_ref[pl.ds(base, PACK_ROWS), o:o + LANES] = val

    _pipelined_groups(rows_hbm, idx_ref, sem, tab_ref, tiles, tt // PEER_GROUP, compute)
    out_ref[...] = _rms(x1_ref[...] + acc_ref[...], gf_ref[...])


def _peer_v_call(rows, coef, x1, gf, v_tab, tt):
    t = x1.shape[0]
    tok = lambda i: (i, 0)
    return pl.pallas_call(
        _peer_v_kernel,
        grid=(t // tt,),
        in_specs=[
            pl.BlockSpec(memory_space=pl.ANY),
            pl.BlockSpec((tt, 4 * PEER_SLOTS), tok),
            pl.BlockSpec((tt, D_MODEL), tok),
            pl.BlockSpec((1, D_MODEL), lambda i: (0, 0)),
            pl.BlockSpec(v_tab.shape, lambda i: (0, 0), pipeline_mode=pl.Buffered(1)),
        ],
        out_specs=pl.BlockSpec((tt, D_MODEL), tok),
        out_shape=jax.ShapeDtypeStruct((t, D_MODEL), F32),
        scratch_shapes=_gather_scratch() + [pltpu.VMEM((tt, D_MODEL), F32)],
        compiler_params=pltpu.CompilerParams(dimension_semantics=("arbitrary",),
                                             vmem_limit_bytes=VMEM_LIMIT_TABLE),
        name="peer_v",
    )(rows, coef, x1, gf, v_tab)


def _pack_rows(tab):
    bits = lax.bitcast_convert_type(tab.astype(BF16), jnp.uint16).astype(jnp.uint32)
    word = bits[:, :D_HALF] | (bits[:, D_HALF:] << 16)
    return lax.bitcast_convert_type(word, I32).reshape(tab.shape[0] * ROW_WORDS, LANES)


def _rope_tables(seq):
    half = DA_HEAD_DIM // 2
    inv = 1.0 / (ROPE_THETA ** (jnp.arange(0, DA_HEAD_DIM, 2, dtype=F32) / DA_HEAD_DIM))
    ang = jnp.arange(seq, dtype=F32)[:, None] * inv[None, :]
    cos, sin = jnp.cos(ang), jnp.sin(ang)
    reps = LANES // DA_HEAD_DIM
    cos_t = jnp.tile(jnp.concatenate([cos, cos], axis=1), (1, reps))
    sin_t = jnp.tile(jnp.concatenate([-sin, sin], axis=1), (1, reps))
    del half
    return cos_t, sin_t


def kernel(x, norm1_g, w_in, lambda_q1, lambda_k1, lambda_q2, lambda_k2, subln_g, sgu_ln_g, sgu_ln_b, w_spatial, b_spatial, w_branch_a, w_branch_b, w_gate, b_gate, w_out, norm2_g, peer_wq, peer_subkeys, peer_u, peer_v, norm_f_g):
    b, s, d = x.shape
    t = b * s
    l = 0
    x2 = x.reshape(t, d)
    cos_t, sin_t = _rope_tables(s)
    w_cat = jnp.concatenate([w_in[l], w_gate[l]], axis=1).astype(BF16)

    tq = min(ATTN_TQ, s)
    q, k0, k1, vt, u_act, sv_ln, gates = _proj_call(
        x2, norm1_g[l][None], w_cat, b_gate[l][None], cos_t, sin_t, sgu_ln_g[l][None], sgu_ln_b[l][None],
        seq=s, tm=tq)

    lam4 = jnp.stack([lambda_q1[l], lambda_k1[l], lambda_q2[l], lambda_k2[l]]).astype(F32)
    ya = _attn_call(lam4, q, k0, k1, vt, subln_g[l][None], batch=b, tq=tq)

    b_s_full = jnp.repeat(b_spatial[l].T, SG_GROUP_DIM, axis=1)
    sk = peer_subkeys[l].reshape(2 * PEER_HEADS, PEER_NKEYS, PEER_HALF).astype(BF16)
    x1, h2, sc = _mix_call(x2, u_act, sv_ln, gates, ya, w_spatial[l], b_s_full,
                           w_branch_a[l].astype(BF16), w_branch_b[l].astype(BF16), w_out[l].astype(BF16),
                           norm2_g[l][None], peer_wq[l].astype(BF16), sk, tm=min(MIX_TM, t))

    rows, g = _topk_call(sc, tt=min(TOPK_TT, t))
    rows = rows.reshape(t // PEER_GROUP, PEER_GROUP, PEER_SLOTS)
    coef = _peer_u_call(rows, h2, g, _pack_rows(peer_u[l]), tt=min(PEER_TT, t))
    out = _peer_v_call(rows, coef, x1, norm_f_g[None], _pack_rows(peer_v[l]), tt=min(PEER_TT, t))
    return out.reshape(b, s, d)
```

```python
import functools
import math

import jax
import jax.numpy as jnp
from jax import lax
from jax.experimental import pallas as pl
from jax.experimental.pallas import tpu as pltpu

F32 = jnp.float32
BF16 = jnp.bfloat16
I32 = jnp.int32

D_MODEL = 1024
CHUNK = 64
ROPE_THETA = 10000.0
EPS = 1e-6

DA_HEADS = 4
DA_HEAD_DIM = 64
DA_V_DIM = 2 * DA_HEAD_DIM
DA_WIDTH = DA_HEADS * DA_V_DIM

SG_GROUPS = 4
SG_BLOCK = 128
SG_GROUP_DIM = 128
SG_WIDTH = SG_GROUPS * SG_GROUP_DIM

PEER_HEADS = 8
PEER_NKEYS = 128
PEER_HALF = 128
PEER_TOPK = 16
PEER_SLOTS = PEER_HEADS * PEER_TOPK
PEER_N_EXPERTS = PEER_NKEYS * PEER_NKEYS

LANES = 128
SUBLANES = 8
PACK_ROWS = 16
PEER_GROUP = 8
D_HALF = D_MODEL // 2
ROW_WORDS = D_HALF // LANES
IDX_SLOTS = 4

ATTN_TQ = 512
MIX_TM = 256
TOPK_TT = 256
PEER_TT = 512

VMEM_LIMIT = 48 * 1024 * 1024
VMEM_LIMIT_TABLE = 56 * 1024 * 1024

LAMBDA_INIT = 0.8 - 0.6 * math.exp(-0.3 * 0)


def _gelu(x):
    return 0.5 * x * (1.0 + lax.erf(x * (0.5 ** 0.5)))


def _rms(x, g):
    return x * lax.rsqrt(jnp.mean(x * x, axis=-1, keepdims=True) + EPS) * g


def _dot(a, b):
    return jnp.dot(a, b, preferred_element_type=F32)


def _dot_t(a, b):
    return lax.dot_general(a, b, (((1,), (1,)), ((), ())), preferred_element_type=F32)


def _proj_kernel(x_ref, g1_ref, w_ref, bg_ref, cos_ref, sin_ref, lng_ref, lnb_ref,
                 q_ref, k0_ref, k1_ref, vt_ref, u_ref, sv_ref, gate_ref):
    tm = x_ref.shape[0]
    hb = _rms(x_ref[...], g1_ref[...]).astype(BF16)
    cos = cos_ref[...]
    sin = sin_ref[...]
    lane = lax.broadcasted_iota(I32, (tm, LANES), 1)
    first_half = (lane % DA_HEAD_DIM) < (DA_HEAD_DIM // 2)

    def rope(t):
        outs = []
        for b in range(DA_WIDTH // LANES):
            blk = t[:, b * LANES:(b + 1) * LANES]
            swapped = jnp.where(first_half, pltpu.roll(blk, LANES - DA_HEAD_DIM // 2, 1),
                                pltpu.roll(blk, DA_HEAD_DIM // 2, 1))
            outs.append(blk * cos + swapped * sin)
        return jnp.concatenate(outs, axis=1)

    o = 0
    q = _dot(hb, w_ref[:, o:o + DA_WIDTH]); o += DA_WIDTH
    q_ref[...] = (rope(q) * (DA_HEAD_DIM ** -0.5)).astype(BF16)
    k = rope(_dot(hb, w_ref[:, o:o + DA_WIDTH])); o += DA_WIDTH
    comp0 = (lax.broadcasted_iota(I32, k.shape, 1) % DA_V_DIM) < DA_HEAD_DIM
    k0_ref[...] = jnp.where(comp0, k, 0.0).astype(BF16)
    k1_ref[...] = jnp.where(comp0, 0.0, k).astype(BF16)
    vt_ref[0] = _dot(hb, w_ref[:, o:o + DA_WIDTH]).T.astype(BF16); o += DA_WIDTH
    u_ref[...] = _gelu(_dot(hb, w_ref[:, o:o + SG_WIDTH])).astype(BF16); o += SG_WIDTH
    sv = _gelu(_dot(hb, w_ref[:, o:o + SG_WIDTH])); o += SG_WIDTH
    mu = jnp.mean(sv, axis=-1, keepdims=True)
    svc = sv - mu
    sv_ref[...] = (svc * lax.rsqrt(jnp.mean(svc * svc, axis=-1, keepdims=True) + EPS) * lng_ref[...]
                   + lnb_ref[...]).astype(BF16)
    gate_ref[...] = jax.nn.sigmoid(_dot(hb, w_ref[:, o:o + 2 * D_MODEL]) + bg_ref[...]).astype(BF16)


def _proj_call(x2, g1, w_cat, b_gate, cos_t, sin_t, ln_g, ln_b, seq, tm):
    t = x2.shape[0]
    nseq = seq // tm
    row = lambda i: (i, 0)
    fixed = lambda i: (0, 0)
    pos = lambda i: (i % nseq, 0)
    wcols = w_cat.shape[1]
    out_shapes = [jax.ShapeDtypeStruct((t, DA_WIDTH), BF16)] * 3 + \
                 [jax.ShapeDtypeStruct((t // tm, DA_WIDTH, tm), BF16)] + \
                 [jax.ShapeDtypeStruct((t, SG_WIDTH), BF16)] * 2 + \
                 [jax.ShapeDtypeStruct((t, 2 * D_MODEL), BF16)]
    return pl.pallas_call(
        _proj_kernel,
        grid=(t // tm,),
        in_specs=[
            pl.BlockSpec((tm, D_MODEL), row),
            pl.BlockSpec((1, D_MODEL), fixed),
            pl.BlockSpec((D_MODEL, wcols), fixed),
            pl.BlockSpec((1, 2 * D_MODEL), fixed),
            pl.BlockSpec((tm, LANES), pos),
            pl.BlockSpec((tm, LANES), pos),
            pl.BlockSpec((1, SG_WIDTH), fixed),
            pl.BlockSpec((1, SG_WIDTH), fixed),
        ],
        out_specs=[pl.BlockSpec((tm, DA_WIDTH), row)] * 3 +
                  [pl.BlockSpec((1, DA_WIDTH, tm), lambda i: (i, 0, 0))] +
                  [pl.BlockSpec((tm, SG_WIDTH), row)] * 2 +
                  [pl.BlockSpec((tm, 2 * D_MODEL), row)],
        out_shape=out_shapes,
        compiler_params=pltpu.CompilerParams(dimension_semantics=("arbitrary",), vmem_limit_bytes=VMEM_LIMIT),
        name="proj",
    )(x2, g1, w_cat, b_gate, cos_t, sin_t, ln_g, ln_b)


def _attn_kernel(lam_ref, q_ref, k0_ref, k1_ref, vt_ref, sg_ref, o_ref, acc_ref, st_ref, m_ref, mprev_ref, l_ref,
                 *, tq):
    i = pl.program_id(2)
    q = q_ref[...]

    m_ref[...] = jnp.full(m_ref.shape, -1e30, F32)
    l_ref[...] = jnp.zeros(l_ref.shape, F32)
    acc_ref[...] = jnp.zeros(acc_ref.shape, F32)

    def scores(j, masked):
        start = pl.multiple_of(j * tq, tq)
        if masked:
            keyc = lax.broadcasted_iota(I32, (tq, tq), 0) // CHUNK
            qryc = lax.broadcasted_iota(I32, (tq, tq), 1) // CHUNK
            visible = keyc <= qryc
        for c, k_ref in enumerate((k0_ref, k1_ref)):
            st = _dot_t(k_ref[pl.ds(start, tq), :], q)
            if masked:
                st = jnp.where(visible, st, -jnp.inf)
            st_ref[c] = st
            m_old = m_ref[c]
            mprev_ref[c] = m_old
            m_ref[c] = jnp.maximum(m_old, jnp.max(st, axis=0, keepdims=True))

    def accumulate(j):
        vt = vt_ref[j]
        for c in range(2):
            m_new = m_ref[c]
            alpha = jnp.exp(mprev_ref[c] - m_new)
            p = jnp.exp(st_ref[c] - m_new)
            l_ref[c] = alpha * l_ref[c] + jnp.sum(p, axis=0, keepdims=True)
            acc_ref[c] = alpha * acc_ref[c] + _dot(vt, p.astype(BF16))

    scores(i, True)

    def body(j, prev):
        accumulate(prev)
        scores(j, False)
        return j

    accumulate(lax.fori_loop(0, i, body, i))

    lam = (jnp.exp(jnp.sum(lam_ref[0:1, :] * lam_ref[1:2, :], axis=-1, keepdims=True))
           - jnp.exp(jnp.sum(lam_ref[2:3, :] * lam_ref[3:4, :], axis=-1, keepdims=True)) + LAMBDA_INIT)
    o = acc_ref[0] / l_ref[0] - lam * (acc_ref[1] / l_ref[1])
    o = o * lax.rsqrt(jnp.mean(o * o, axis=0, keepdims=True) + EPS)
    o_ref[...] = (o.T * sg_ref[...] * (1.0 - LAMBDA_INIT)).astype(BF16)


def _attn_call(lam4, q, k0, k1, vt, subln_g, batch, tq):
    t = q.shape[0]
    s = t // batch
    nq = s // tq
    kern = functools.partial(_attn_kernel, tq=tq)
    qblk = pl.BlockSpec((tq, DA_V_DIM), lambda bi, h, i: (bi * nq + i, h))
    kblk = pl.BlockSpec((s, DA_V_DIM), lambda bi, h, i: (bi, h))
    stat = pltpu.VMEM((2, 1, tq), F32)
    return pl.pallas_call(
        kern,
        grid=(batch, DA_HEADS, nq),
        in_specs=[
            pl.BlockSpec((4, DA_HEAD_DIM), lambda bi, h, i: (0, 0)),
            qblk, kblk, kblk,
            pl.BlockSpec((nq, DA_V_DIM, tq), lambda bi, h, i: (bi, h, 0)),
            pl.BlockSpec((1, DA_V_DIM), lambda bi, h, i: (0, 0)),
        ],
        out_specs=qblk,
        out_shape=jax.ShapeDtypeStruct((t, DA_WIDTH), BF16),
        scratch_shapes=[
            pltpu.VMEM((2, DA_V_DIM, tq), F32),
            pltpu.VMEM((2, tq, tq), F32),
            stat, stat, stat,
        ],
        compiler_params=pltpu.CompilerParams(dimension_semantics=("arbitrary",) * 3, vmem_limit_bytes=VMEM_LIMIT),
        name="attn",
    )(lam4, q, k0, k1, vt, subln_g)


def _mix_kernel(x_ref, u_ref, sv_ref, gate_ref, ya_ref, ws_ref, bs_ref, wa_ref, wb_ref, wo_ref, g2_ref,
                wq_ref, sk_ref, x1_ref, h2_ref, sc_ref):
    tm = x_ref.shape[0]
    pc = lax.broadcasted_iota(I32, (SG_BLOCK, SG_BLOCK), 0) // CHUNK
    pr = lax.broadcasted_iota(I32, (SG_BLOCK, SG_BLOCK), 1) // CHUNK
    causal = pr <= pc
    rows = []
    for n in range(tm // SG_BLOCK):
        cols = []
        for g in range(SG_GROUPS):
            w = jnp.where(causal, ws_ref[g], 0.0).astype(BF16)
            vb = sv_ref[n * SG_BLOCK:(n + 1) * SG_BLOCK, g * SG_GROUP_DIM:(g + 1) * SG_GROUP_DIM]
            cols.append(_dot(w, vb))
        rows.append(jnp.concatenate(cols, axis=1) + bs_ref[...])
    s = jnp.concatenate(rows, axis=0)
    yb = (u_ref[...].astype(F32) * s).astype(BF16)
    merged = (gate_ref[:, :D_MODEL].astype(F32) * _dot(ya_ref[...], wa_ref[...])
              + gate_ref[:, D_MODEL:].astype(F32) * _dot(yb, wb_ref[...]))
    x1 = x_ref[...] + _dot(merged.astype(BF16), wo_ref[...])
    x1_ref[...] = x1
    h2 = _rms(x1, g2_ref[...]).astype(BF16)
    h2_ref[...] = h2
    qp = _dot(h2, wq_ref[...]).astype(BF16)
    for hp in range(2 * PEER_HEADS):
        sc_ref[hp] = _dot_t(sk_ref[hp], qp[:, hp * PEER_HALF:(hp + 1) * PEER_HALF])


def _mix_call(x2, u_act, sv_ln, gates, ya, w_s, b_s_full, w_a, w_b, w_o, g2, w_q, sk, tm):
    t = x2.shape[0]
    row = lambda i: (i, 0)
    fixed2 = lambda i: (0, 0)
    fixed3 = lambda i: (0, 0, 0)
    return pl.pallas_call(
        _mix_kernel,
        grid=(t // tm,),
        in_specs=[
            pl.BlockSpec((tm, D_MODEL), row),
            pl.BlockSpec((tm, SG_WIDTH), row),
            pl.BlockSpec((tm, SG_WIDTH), row),
            pl.BlockSpec((tm, 2 * D_MODEL), row),
            pl.BlockSpec((tm, DA_WIDTH), row),
            pl.BlockSpec((SG_GROUPS, SG_BLOCK, SG_BLOCK), fixed3),
            pl.BlockSpec((SG_BLOCK, SG_WIDTH), fixed2),
            pl.BlockSpec((DA_WIDTH, D_MODEL), fixed2),
            pl.BlockSpec((SG_WIDTH, D_MODEL), fixed2),
            pl.BlockSpec((D_MODEL, D_MODEL), fixed2),
            pl.BlockSpec((1, D_MODEL), fixed2),
            pl.BlockSpec((D_MODEL, 2 * PEER_HEADS * PEER_HALF), fixed2),
            pl.BlockSpec((2 * PEER_HEADS, PEER_NKEYS, PEER_HALF), fixed3),
        ],
        out_specs=[
            pl.BlockSpec((tm, D_MODEL), row),
            pl.BlockSpec((tm, D_MODEL), row),
            pl.BlockSpec((2 * PEER_HEADS, PEER_NKEYS, tm), lambda i: (0, 0, i)),
        ],
        out_shape=[
            jax.ShapeDtypeStruct((t, D_MODEL), F32),
            jax.ShapeDtypeStruct((t, D_MODEL), BF16),
            jax.ShapeDtypeStruct((2 * PEER_HEADS, PEER_NKEYS, t), F32),
        ],
        compiler_params=pltpu.CompilerParams(dimension_semantics=("arbitrary",), vmem_limit_bytes=VMEM_LIMIT),
        name="mix",
    )(x2, u_act, sv_ln, gates, ya, w_s, b_s_full, w_a, w_b, w_o, g2, w_q, sk)


def _top16(vals, n):
    tt = vals.shape[1]
    pos = lax.broadcasted_iota(I32, (n, tt), 0)
    cur = vals
    out_v, out_i = [], []
    for _ in range(PEER_TOPK):
        m = jnp.max(cur, axis=0, keepdims=True)
        idx = jnp.min(jnp.where(cur == m, pos, n), axis=0, keepdims=True)
        out_v.append(m)
        out_i.append(idx)
        cur = jnp.where(pos == idx, -jnp.inf, cur)
    return out_v, out_i


def _topk_kernel(sc_ref, row_ref, g_ref):
    tt = sc_ref.shape[2]
    k = PEER_TOPK
    rows_all, g_all = [], []
    for h in range(PEER_HEADS):
        s1, i1 = _top16(sc_ref[2 * h], PEER_NKEYS)
        s2, i2 = _top16(sc_ref[2 * h + 1], PEER_NKEYS)
        s2m = jnp.concatenate(s2, axis=0)
        i2m = jnp.concatenate(i2, axis=0)
        keep = [k // (a + 1) for a in range(k)]
        n_cand = -(-sum(keep) // SUBLANES) * SUBLANES
        pad = n_cand - sum(keep)
        cand = jnp.concatenate([s1[a] + s2m[:keep[a]] for a in range(k)]
                               + [jnp.full((pad, tt), -jnp.inf, F32)], axis=0)
        cidx = jnp.concatenate([i1[a] * PEER_NKEYS + i2m[:keep[a]] for a in range(k)]
                               + [jnp.zeros((pad, tt), I32)], axis=0)
        top_s, pos = _top16(cand, n_cand)
        cpos = lax.broadcasted_iota(I32, (n_cand, tt), 0)
        e = [jnp.sum(jnp.where(cpos == pos[r], cidx, 0), axis=0, keepdims=True) for r in range(k)]
        w = [jnp.exp(top_s[r] - top_s[0]) for r in range(k)]
        den = w[0]
        for r in range(1, k):
            den = den + w[r]
        rows_all.append(jnp.concatenate(e, axis=0) * ROW_WORDS)
        g_all.append(jnp.concatenate(w, axis=0) / den)
    row_ref[...] = jnp.concatenate(rows_all, axis=0).T
    g_ref[...] = jnp.concatenate(g_all, axis=0).T


def _topk_call(sc, tt):
    t = sc.shape[2]
    blk = pl.BlockSpec((tt, PEER_SLOTS), lambda i: (i, 0))
    return pl.pallas_call(
        _topk_kernel,
        grid=(t // tt,),
        in_specs=[pl.BlockSpec((2 * PEER_HEADS, PEER_NKEYS, tt), lambda i: (0, 0, i))],
        out_specs=[blk, blk],
        out_shape=[jax.ShapeDtypeStruct((t, PEER_SLOTS), I32),
                   jax.ShapeDtypeStruct((t, PEER_SLOTS), F32)],
        compiler_params=pltpu.CompilerParams(dimension_semantics=("arbitrary",), vmem_limit_bytes=VMEM_LIMIT),
        name="topk",
    )(sc)


def _chunk_operand(tile_ref, c):
    return pltpu.bitcast(tile_ref[pl.ds(c, PEER_SLOTS, stride=ROW_WORDS), :], BF16)


def _gather_group(idx_ref, slot, tab_ref, tiles):
    for s in range(PEER_GROUP):
        for j in range(PEER_SLOTS):
            r = pl.multiple_of(idx_ref[slot, s, j], ROW_WORDS)
            tiles[s][pl.ds(j * ROW_WORDS, ROW_WORDS), :] = tab_ref[pl.ds(r, ROW_WORDS), :]


def _pipelined_groups(rows_hbm, idx_ref, sem, tab_ref, tiles, n_groups, compute):
    first = pl.program_id(0) * n_groups

    def idx_copy(g, slot):
        src = rows_hbm.at[first + jnp.minimum(g, n_groups - 1)]
        return pltpu.make_async_copy(src, idx_ref.at[slot], sem.at[slot])

    for g in range(IDX_SLOTS - 1):
        idx_copy(g, g).start()
    idx_copy(0, 0).wait()
    _gather_group(idx_ref, 0, tab_ref, tiles)
    idx_copy(IDX_SLOTS - 1, IDX_SLOTS - 1).start()

    def body(it, carry):
        for k in range(IDX_SLOTS):
            g = IDX_SLOTS * it + k
            nxt = (k + 1) % IDX_SLOTS
            idx_copy(g + 1, nxt).wait()
            compute(g)
            _gather_group(idx_ref, nxt, tab_ref, tiles)
            idx_copy(g + IDX_SLOTS, k).start()
        return carry

    lax.fori_loop(0, n_groups // IDX_SLOTS, body, 0)
    for slot in range(1, IDX_SLOTS):
        idx_copy(n_groups, slot).wait()


def _group_rows(g):
    t0 = g * PEER_GROUP
    base = pl.multiple_of((t0 // PACK_ROWS) * PACK_ROWS, PACK_ROWS)
    return base, t0 - base


def _peer_u_kernel(rows_hbm, h2_ref, g_ref, tab_ref, coef_ref, idx_ref, sem, *scratch):
    tiles, acc_lo, acc_hi = scratch[:PEER_GROUP], scratch[PEER_GROUP], scratch[PEER_GROUP + 1]
    tt = h2_ref.shape[0]
    width = 2 * PEER_SLOTS
    sub = lax.broadcasted_iota(I32, (PACK_ROWS, width), 0)
    acc_lo[...] = jnp.zeros(acc_lo.shape, F32)
    acc_hi[...] = jnp.zeros(acc_hi.shape, F32)

    def compute(g):
        base, r0 = _group_rows(g)
        xg = h2_ref[pl.ds(base, PACK_ROWS), :]
        lhs = [jnp.concatenate([xg[:, c * LANES:(c + 1) * LANES],
                                xg[:, D_HALF + c * LANES:D_HALF + (c + 1) * LANES]], axis=0)
               for c in range(ROW_WORDS)]
        lo = acc_lo[pl.ds(base, PACK_ROWS), :]
        hi = acc_hi[pl.ds(base, PACK_ROWS), :]
        for s in range(PEER_GROUP):
            res = _dot_t(lhs[0], _chunk_operand(tiles[s], 0))
            for c in range(1, ROW_WORDS):
                res = res + _dot_t(lhs[c], _chunk_operand(tiles[s], c))
            lo = jnp.where(sub == r0 + s, res[:PACK_ROWS], lo)
            hi = jnp.where(sub == r0 + s, res[PACK_ROWS:], hi)
        acc_lo[pl.ds(base, PACK_ROWS), :] = lo
        acc_hi[pl.ds(base, PACK_ROWS), :] = hi

    _pipelined_groups(rows_hbm, idx_ref, sem, tab_ref, tiles, tt // PEER_GROUP, compute)
    col = lax.broadcasted_iota(I32, (tt, width), 1)
    halves = jnp.where(col % 2 == 0, acc_lo[...], acc_hi[...])
    act = _gelu(halves + pltpu.roll(halves, width - 1, 1))
    src = lax.broadcasted_iota(I32, (PEER_SLOTS, width), 0)
    dst = lax.broadcasted_iota(I32, (PEER_SLOTS, width), 1)
    spread = jnp.where(dst == 2 * src, 1.0, 0.0).astype(BF16)
    coef = act * _dot(g_ref[...].astype(BF16), spread)
    coef_ref[:, :width] = coef.astype(BF16)
    coef_ref[:, width:] = pltpu.roll(coef, 1, 1).astype(BF16)


def _gather_scratch():
    return ([pltpu.SMEM((IDX_SLOTS, PEER_GROUP, PEER_SLOTS), I32), pltpu.SemaphoreType.DMA((IDX_SLOTS,))]
            + [pltpu.VMEM((ROW_WORDS * PEER_SLOTS, LANES), I32) for _ in range(PEER_GROUP)])


def _peer_u_call(rows, h2, g, u_tab, tt):
    t = h2.shape[0]
    tok = lambda i: (i, 0)
    width = 2 * PEER_SLOTS
    return pl.pallas_call(
        _peer_u_kernel,
        grid=(t // tt,),
        in_specs=[
            pl.BlockSpec(memory_space=pl.ANY),
            pl.BlockSpec((tt, D_MODEL), tok),
            pl.BlockSpec((tt, PEER_SLOTS), tok),
            pl.BlockSpec(u_tab.shape, lambda i: (0, 0), pipeline_mode=pl.Buffered(1)),
        ],
        out_specs=pl.BlockSpec((tt, 2 * width), tok),
        out_shape=jax.ShapeDtypeStruct((t, 2 * width), BF16),
        scratch_shapes=_gather_scratch() + [pltpu.VMEM((tt, width), F32)] * 2,
        compiler_params=pltpu.CompilerParams(dimension_semantics=("arbitrary",),
                                             vmem_limit_bytes=VMEM_LIMIT_TABLE),
        name="peer_u",
    )(rows, h2, g, u_tab)


def _peer_v_kernel(rows_hbm, coef_ref, x1_ref, gf_ref, tab_ref, out_ref, idx_ref, sem, *scratch):
    tiles, acc_ref = scratch[:PEER_GROUP], scratch[PEER_GROUP]
    tt = coef_ref.shape[0]
    width = 2 * PEER_SLOTS
    sub = lax.broadcasted_iota(I32, (PACK_ROWS, LANES), 0)
    acc_ref[...] = jnp.zeros(acc_ref.shape, F32)

    def compute(g):
        base, r0 = _group_rows(g)
        cg = jnp.concatenate([coef_ref[pl.ds(base, PACK_ROWS), :width],
                              coef_ref[pl.ds(base, PACK_ROWS), width:]], axis=0)
        cols = [h * D_HALF + c * LANES for c in range(ROW_WORDS) for h in range(2)]
        outs = [acc_ref[pl.ds(base, PACK_ROWS), o:o + LANES] for o in cols]
        for s in range(PEER_GROUP):
            for c in range(ROW_WORDS):
                res = _dot(cg, _chunk_operand(tiles[s], c))
                outs[2 * c] = jnp.where(sub == r0 + s, res[:PACK_ROWS], outs[2 * c])
                outs[2 * c + 1] = jnp.where(sub == r0 + s, res[PACK_ROWS:], outs[2 * c + 1])
        for o, val in zip(cols, outs):
            acc_ref[pl.ds(base, PACK_ROWS), o:o + LANES] = val

    _pipelined_groups(rows_hbm, idx_ref, sem, tab_ref, tiles, tt // PEER_GROUP, compute)
    out_ref[...] = _rms(x1_ref[...] + acc_ref[...], gf_ref[...])


def _peer_v_call(rows, coef, x1, gf, v_tab, tt):
    t = x1.shape[0]
    tok = lambda i: (i, 0)
    return pl.pallas_call(
        _peer_v_kernel,
        grid=(t // tt,),
        in_specs=[
            pl.BlockSpec(memory_space=pl.ANY),
            pl.BlockSpec((tt, 4 * PEER_SLOTS), tok),
            pl.BlockSpec((tt, D_MODEL), tok),
            pl.BlockSpec((1, D_MODEL), lambda i: (0, 0)),
            pl.BlockSpec(v_tab.shape, lambda i: (0, 0), pipeline_mode=pl.Buffered(1)),
        ],
        out_specs=pl.BlockSpec((tt, D_MODEL), tok),
        out_shape=jax.ShapeDtypeStruct((t, D_MODEL), F32),
        scratch_shapes=_gather_scratch() + [pltpu.VMEM((tt, D_MODEL), F32)],
        compiler_params=pltpu.CompilerParams(dimension_semantics=("arbitrary",),
                                             vmem_limit_bytes=VMEM_LIMIT_TABLE),
        name="peer_v",
    )(rows, coef, x1, gf, v_tab)


def _pack_rows(tab):
    bits = lax.bitcast_convert_type(tab.astype(BF16), jnp.uint16).astype(jnp.uint32)
    word = bits[:, :D_HALF] | (bits[:, D_HALF:] << 16)
    return lax.bitcast_convert_type(word, I32).reshape(tab.shape[0] * ROW_WORDS, LANES)


def _rope_tables(seq):
    half = DA_HEAD_DIM // 2
    inv = 1.0 / (ROPE_THETA ** (jnp.arange(0, DA_HEAD_DIM, 2, dtype=F32) / DA_HEAD_DIM))
    ang = jnp.arange(seq, dtype=F32)[:, None] * inv[None, :]
    cos, sin = jnp.cos(ang), jnp.sin(ang)
    reps = LANES // DA_HEAD_DIM
    cos_t = jnp.tile(jnp.concatenate([cos, cos], axis=1), (1, reps))
    sin_t = jnp.tile(jnp.concatenate([-sin, sin], axis=1), (1, reps))
    del half
    return cos_t, sin_t


def kernel(x, norm1_g, w_in, lambda_q1, lambda_k1, lambda_q2, lambda_k2, subln_g, sgu_ln_g, sgu_ln_b, w_spatial, b_spatial, w_branch_a, w_branch_b, w_gate, b_gate, w_out, norm2_g, peer_wq, peer_subkeys, peer_u, peer_v, norm_f_g):
    b, s, d = x.shape
    t = b * s
    l = 0
    x2 = x.reshape(t, d)
    cos_t, sin_t = _rope_tables(s)
    w_cat = jnp.concatenate([w_in[l], w_gate[l]], axis=1).astype(BF16)

    tq = min(ATTN_TQ, s)
    q, k0, k1, vt, u_act, sv_ln, gates = _proj_call(
        x2, norm1_g[l][None], w_cat, b_gate[l][None], cos_t, sin_t, sgu_ln_g[l][None], sgu_ln_b[l][None],
        seq=s, tm=tq)

    lam4 = jnp.stack([lambda_q1[l], lambda_k1[l], lambda_q2[l], lambda_k2[l]]).astype(F32)
    ya = _attn_call(lam4, q, k0, k1, vt, subln_g[l][None], batch=b, tq=tq)

    b_s_full = jnp.repeat(b_spatial[l].T, SG_GROUP_DIM, axis=1)
    sk = peer_subkeys[l].reshape(2 * PEER_HEADS, PEER_NKEYS, PEER_HALF).astype(BF16)
    x1, h2, sc = _mix_call(x2, u_act, sv_ln, gates, ya, w_spatial[l], b_s_full,
                           w_branch_a[l].astype(BF16), w_branch_b[l].astype(BF16), w_out[l].astype(BF16),
                           norm2_g[l][None], peer_wq[l].astype(BF16), sk, tm=min(MIX_TM, t))

    rows, g = _topk_call(sc, tt=min(TOPK_TT, t))
    rows = rows.reshape(t // PEER_GROUP, PEER_GROUP, PEER_SLOTS)
    coef = _peer_u_call(rows, h2, g, _pack_rows(peer_u[l]), tt=min(PEER_TT, t))
    out = _peer_v_call(rows, coef, x1, norm_f_g[None], _pack_rows(peer_v[l]), tt=min(PEER_TT, t))
    return out.reshape(b, s, d)
```

```python
import functools
import math

import jax
import jax.numpy as jnp
from jax import lax
from jax.experimental import pallas as pl
from jax.experimental.pallas import tpu as pltpu

F32 = jnp.float32
BF16 = jnp.bfloat16
I32 = jnp.int32

D_MODEL = 1024
CHUNK = 64
ROPE_THETA = 10000.0
EPS = 1e-6

DA_HEADS = 4
DA_HEAD_DIM = 64
DA_V_DIM = 2 * DA_HEAD_DIM
DA_WIDTH = DA_HEADS * DA_V_DIM

SG_GROUPS = 4
SG_BLOCK = 128
SG_GROUP_DIM = 128
SG_WIDTH = SG_GROUPS * SG_GROUP_DIM

PEER_HEADS = 8
PEER_NKEYS = 128
PEER_HALF = 128
PEER_TOPK = 16
PEER_SLOTS = PEER_HEADS * PEER_TOPK
PEER_N_EXPERTS = PEER_NKEYS * PEER_NKEYS

LANES = 128
SUBLANES = 8
PACK_ROWS = 16
PEER_GROUP = 8
D_HALF = D_MODEL // 2
ROW_WORDS = D_HALF // LANES
IDX_SLOTS = 4

ATTN_TQ = 1024
MIX_TM = 256
TOPK_TT = 256
PEER_TT = 512

VMEM_LIMIT = 48 * 1024 * 1024
VMEM_LIMIT_TABLE = 56 * 1024 * 1024

LAMBDA_INIT = 0.8 - 0.6 * math.exp(-0.3 * 0)
LOG2E = math.log2(math.e)


def _gelu(x):
    return 0.5 * x * (1.0 + lax.erf(x * (0.5 ** 0.5)))


def _rms(x, g):
    return x * lax.rsqrt(jnp.mean(x * x, axis=-1, keepdims=True) + EPS) * g


def _dot(a, b):
    return jnp.dot(a, b, preferred_element_type=F32)


def _dot_t(a, b):
    return lax.dot_general(a, b, (((1,), (1,)), ((), ())), preferred_element_type=F32)


def _proj_kernel(x_ref, g1_ref, w_ref, bg_ref, cos_ref, sin_ref, lng_ref, lnb_ref,
                 qt_ref, k0_ref, k1_ref, vt_ref, u_ref, sv_ref, gate_ref):
    tm = x_ref.shape[0]
    hb = _rms(x_ref[...], g1_ref[...]).astype(BF16)
    cos = cos_ref[...]
    sin = sin_ref[...]
    lane = lax.broadcasted_iota(I32, (tm, LANES), 1)
    first_half = (lane % DA_HEAD_DIM) < (DA_HEAD_DIM // 2)

    def rope(t):
        outs = []
        for b in range(DA_WIDTH // LANES):
            blk = t[:, b * LANES:(b + 1) * LANES]
            swapped = jnp.where(first_half, pltpu.roll(blk, LANES - DA_HEAD_DIM // 2, 1),
                                pltpu.roll(blk, DA_HEAD_DIM // 2, 1))
            outs.append(blk * cos + swapped * sin)
        return jnp.concatenate(outs, axis=1)

    o = 0
    q = _dot(hb, w_ref[:, o:o + DA_WIDTH]); o += DA_WIDTH
    qt_ref[0] = (rope(q) * (DA_HEAD_DIM ** -0.5 * LOG2E)).T.astype(BF16)
    k = rope(_dot(hb, w_ref[:, o:o + DA_WIDTH])); o += DA_WIDTH
    comp0 = (lax.broadcasted_iota(I32, k.shape, 1) % DA_V_DIM) < DA_HEAD_DIM
    k0_ref[...] = jnp.where(comp0, k, 0.0).astype(BF16)
    k1_ref[...] = jnp.where(comp0, 0.0, k).astype(BF16)
    vt_ref[0] = _dot(hb, w_ref[:, o:o + DA_WIDTH]).T.astype(BF16); o += DA_WIDTH
    u_ref[...] = _gelu(_dot(hb, w_ref[:, o:o + SG_WIDTH])).astype(BF16); o += SG_WIDTH
    sv = _gelu(_dot(hb, w_ref[:, o:o + SG_WIDTH])); o += SG_WIDTH
    mu = jnp.mean(sv, axis=-1, keepdims=True)
    svc = sv - mu
    sv_ref[...] = (svc * lax.rsqrt(jnp.mean(svc * svc, axis=-1, keepdims=True) + EPS) * lng_ref[...]
                   + lnb_ref[...]).astype(BF16)
    gate_ref[...] = jax.nn.sigmoid(_dot(hb, w_ref[:, o:o + 2 * D_MODEL]) + bg_ref[...]).astype(BF16)


def _proj_call(x2, g1, w_cat, b_gate, cos_t, sin_t, ln_g, ln_b, seq, tm):
    t = x2.shape[0]
    nseq = seq // tm
    row = lambda i: (i, 0)
    fixed = lambda i: (0, 0)
    pos = lambda i: (i % nseq, 0)
    wcols = w_cat.shape[1]
    transposed = jax.ShapeDtypeStruct((t // tm, DA_WIDTH, tm), BF16)
    tblk = pl.BlockSpec((1, DA_WIDTH, tm), lambda i: (i, 0, 0))
    out_shapes = [transposed] + [jax.ShapeDtypeStruct((t, DA_WIDTH), BF16)] * 2 + [transposed] + \
                 [jax.ShapeDtypeStruct((t, SG_WIDTH), BF16)] * 2 + \
                 [jax.ShapeDtypeStruct((t, 2 * D_MODEL), BF16)]
    return pl.pallas_call(
        _proj_kernel,
        grid=(t // tm,),
        in_specs=[
            pl.BlockSpec((tm, D_MODEL), row),
            pl.BlockSpec((1, D_MODEL), fixed),
            pl.BlockSpec((D_MODEL, wcols), fixed, pipeline_mode=pl.Buffered(1)),
            pl.BlockSpec((1, 2 * D_MODEL), fixed),
            pl.BlockSpec((tm, LANES), pos),
            pl.BlockSpec((tm, LANES), pos),
            pl.BlockSpec((1, SG_WIDTH), fixed),
            pl.BlockSpec((1, SG_WIDTH), fixed),
        ],
        out_specs=[tblk] + [pl.BlockSpec((tm, DA_WIDTH), row)] * 2 + [tblk] +
                  [pl.BlockSpec((tm, SG_WIDTH), row)] * 2 +
                  [pl.BlockSpec((tm, 2 * D_MODEL), row)],
        out_shape=out_shapes,
        compiler_params=pltpu.CompilerParams(dimension_semantics=("arbitrary",), vmem_limit_bytes=VMEM_LIMIT),
        name="proj",
    )(x2, g1, w_cat, b_gate, cos_t, sin_t, ln_g, ln_b)


def _attn_kernel(lam_ref, qt_ref, k0_ref, k1_ref, vt_ref, sg_ref, o_ref, acc_ref, st_ref, m_ref, mprev_ref, l_ref,
                 *, tq):
    i = pl.program_id(2)
    qt = qt_ref[0]

    m_ref[...] = jnp.full(m_ref.shape, -1e30, F32)
    l_ref[...] = jnp.zeros(l_ref.shape, F32)
    acc_ref[...] = jnp.zeros(acc_ref.shape, F32)

    def scores(j, masked):
        start = pl.multiple_of(j * tq, tq)
        if masked:
            keyc = lax.broadcasted_iota(I32, (tq, tq), 0) // CHUNK
            qryc = lax.broadcasted_iota(I32, (tq, tq), 1) // CHUNK
            visible = keyc <= qryc
        for c, k_ref in enumerate((k0_ref, k1_ref)):
            st = _dot(k_ref[pl.ds(start, tq), :], qt)
            if masked:
                st = jnp.where(visible, st, -jnp.inf)
            st_ref[c] = st
            m_old = m_ref[c]
            mprev_ref[c] = m_old
            m_ref[c] = jnp.maximum(m_old, jnp.max(st, axis=0, keepdims=True))

    def accumulate(j):
        vt = vt_ref[j]
        for c in range(2):
            m_new = m_ref[c]
            alpha = jnp.exp2(mprev_ref[c] - m_new)
            p = jnp.exp2(st_ref[c] - m_new)
            l_ref[c] = alpha * l_ref[c] + jnp.sum(p, axis=0, keepdims=True)
            acc_ref[c] = alpha * acc_ref[c] + _dot(vt, p.astype(BF16))

    scores(i, True)

    def body(j, prev):
        accumulate(prev)
        scores(j, False)
        return j

    accumulate(lax.fori_loop(0, i, body, i))

    lam = (jnp.exp(jnp.sum(lam_ref[0:1, :] * lam_ref[1:2, :], axis=-1, keepdims=True))
           - jnp.exp(jnp.sum(lam_ref[2:3, :] * lam_ref[3:4, :], axis=-1, keepdims=True)) + LAMBDA_INIT)
    o = acc_ref[0] / l_ref[0] - lam * (acc_ref[1] / l_ref[1])
    o = o * lax.rsqrt(jnp.mean(o * o, axis=0, keepdims=True) + EPS)
    o_ref[...] = (o.T * sg_ref[...] * (1.0 - LAMBDA_INIT)).astype(BF16)


def _attn_call(lam4, qt, k0, k1, vt, subln_g, batch, tq):
    t = k0.shape[0]
    s = t // batch
    nq = s // tq
    kern = functools.partial(_attn_kernel, tq=tq)
    kblk = pl.BlockSpec((s, DA_V_DIM), lambda bi, h, i: (bi, h))
    stat = pltpu.VMEM((2, 1, tq), F32)
    return pl.pallas_call(
        kern,
        grid=(batch, DA_HEADS, nq),
        in_specs=[
            pl.BlockSpec((4, DA_HEAD_DIM), lambda bi, h, i: (0, 0)),
            pl.BlockSpec((1, DA_V_DIM, tq), lambda bi, h, i: (bi * nq + i, h, 0)),
            kblk, kblk,
            pl.BlockSpec((nq, DA_V_DIM, tq), lambda bi, h, i: (bi, h, 0)),
            pl.BlockSpec((1, DA_V_DIM), lambda bi, h, i: (0, 0)),
        ],
        out_specs=pl.BlockSpec((tq, DA_V_DIM), lambda bi, h, i: (bi * nq + i, h)),
        out_shape=jax.ShapeDtypeStruct((t, DA_WIDTH), BF16),
        scratch_shapes=[
            pltpu.VMEM((2, DA_V_DIM, tq), F32),
            pltpu.VMEM((2, tq, tq), F32),
            stat, stat, stat,
        ],
        compiler_params=pltpu.CompilerParams(dimension_semantics=("arbitrary",) * 3, vmem_limit_bytes=VMEM_LIMIT),
        name="attn",
    )(lam4, qt, k0, k1, vt, subln_g)


def _mix_kernel(x_ref, u_ref, sv_ref, gate_ref, ya_ref, ws_ref, bs_ref, wa_ref, wb_ref, wo_ref, g2_ref,
                wq_ref, sk_ref, x1_ref, h2_ref, sc_ref):
    tm = x_ref.shape[0]
    pc = lax.broadcasted_iota(I32, (SG_BLOCK, SG_BLOCK), 0) // CHUNK
    pr = lax.broadcasted_iota(I32, (SG_BLOCK, SG_BLOCK), 1) // CHUNK
    causal = pr <= pc
    rows = []
    for n in range(tm // SG_BLOCK):
        cols = []
        for g in range(SG_GROUPS):
            w = jnp.where(causal, ws_ref[g], 0.0).astype(BF16)
            vb = sv_ref[n * SG_BLOCK:(n + 1) * SG_BLOCK, g * SG_GROUP_DIM:(g + 1) * SG_GROUP_DIM]
            cols.append(_dot(w, vb))
        rows.append(jnp.concatenate(cols, axis=1) + bs_ref[...])
    s = jnp.concatenate(rows, axis=0)
    yb = (u_ref[...].astype(F32) * s).astype(BF16)
    merged = (gate_ref[:, :D_MODEL].astype(F32) * _dot(ya_ref[...], wa_ref[...])
              + gate_ref[:, D_MODEL:].astype(F32) * _dot(yb, wb_ref[...]))
    x1 = x_ref[...] + _dot(merged.astype(BF16), wo_ref[...])
    x1_ref[...] = x1
    h2 = _rms(x1, g2_ref[...]).astype(BF16)
    h2_ref[...] = h2
    qp = _dot(h2, wq_ref[...]).astype(BF16)
    for hp in range(2 * PEER_HEADS):
        sc_ref[hp] = _dot_t(sk_ref[hp], qp[:, hp * PEER_HALF:(hp + 1) * PEER_HALF])


def _mix_call(x2, u_act, sv_ln, gates, ya, w_s, b_s_full, w_a, w_b, w_o, g2, w_q, sk, tm):
    t = x2.shape[0]
    row = lambda i: (i, 0)
    fixed2 = lambda i: (0, 0)
    fixed3 = lambda i: (0, 0, 0)
    return pl.pallas_call(
        _mix_kernel,
        grid=(t // tm,),
        in_specs=[
            pl.BlockSpec((tm, D_MODEL), row),
            pl.BlockSpec((tm, SG_WIDTH), row),
            pl.BlockSpec((tm, SG_WIDTH), row),
            pl.BlockSpec((tm, 2 * D_MODEL), row),
            pl.BlockSpec((tm, DA_WIDTH), row),
            pl.BlockSpec((SG_GROUPS, SG_BLOCK, SG_BLOCK), fixed3),
            pl.BlockSpec((SG_BLOCK, SG_WIDTH), fixed2),
            pl.BlockSpec((DA_WIDTH, D_MODEL), fixed2),
            pl.BlockSpec((SG_WIDTH, D_MODEL), fixed2),
            pl.BlockSpec((D_MODEL, D_MODEL), fixed2),
            pl.BlockSpec((1, D_MODEL), fixed2),
            pl.BlockSpec((D_MODEL, 2 * PEER_HEADS * PEER_HALF), fixed2),
            pl.BlockSpec((2 * PEER_HEADS, PEER_NKEYS, PEER_HALF), fixed3),
        ],
        out_specs=[
            pl.BlockSpec((tm, D_MODEL), row),
            pl.BlockSpec((tm, D_MODEL), row),
            pl.BlockSpec((2 * PEER_HEADS, PEER_NKEYS, tm), lambda i: (0, 0, i)),
        ],
        out_shape=[
            jax.ShapeDtypeStruct((t, D_MODEL), F32),
            jax.ShapeDtypeStruct((t, D_MODEL), BF16),
            jax.ShapeDtypeStruct((2 * PEER_HEADS, PEER_NKEYS, t), F32),
        ],
        compiler_params=pltpu.CompilerParams(dimension_semantics=("arbitrary",), vmem_limit_bytes=VMEM_LIMIT),
        name="mix",
    )(x2, u_act, sv_ln, gates, ya, w_s, b_s_full, w_a, w_b, w_o, g2, w_q, sk)


def _top16(vals, n):
    tt = vals.shape[1]
    pos = lax.broadcasted_iota(I32, (n, tt), 0)
    cur = vals
    out_v, out_i = [], []
    for _ in range(PEER_TOPK):
        m = jnp.max(cur, axis=0, keepdims=True)
        idx = jnp.min(jnp.where(cur == m, pos, n), axis=0, keepdims=True)
        out_v.append(m)
        out_i.append(idx)
        cur = jnp.where(pos == idx, -jnp.inf, cur)
    return out_v, out_i


def _topk_kernel(sc_ref, row_ref, g_ref):
    tt = sc_ref.shape[2]
    k = PEER_TOPK
    rows_all, g_all = [], []
    for h in range(PEER_HEADS):
        s1, i1 = _top16(sc_ref[2 * h], PEER_NKEYS)
        s2, i2 = _top16(sc_ref[2 * h + 1], PEER_NKEYS)
        s2m = jnp.concatenate(s2, axis=0)
        i2m = jnp.concatenate(i2, axis=0)
        keep = [k // (a + 1) for a in range(k)]
        n_cand = -(-sum(keep) // SUBLANES) * SUBLANES
        pad = n_cand - sum(keep)
        cand = jnp.concatenate([s1[a] + s2m[:keep[a]] for a in range(k)]
                               + [jnp.full((pad, tt), -jnp.inf, F32)], axis=0)
        cidx = jnp.concatenate([i1[a] * PEER_NKEYS + i2m[:keep[a]] for a in range(k)]
                               + [jnp.zeros((pad, tt), I32)], axis=0)
        top_s, pos = _top16(cand, n_cand)
        cpos = lax.broadcasted_iota(I32, (n_cand, tt), 0)
        e = [jnp.sum(jnp.where(cpos == pos[r], cidx, 0), axis=0, keepdims=True) for r in range(k)]
        w = [jnp.exp(top_s[r] - top_s[0]) for r in range(k)]
        den = w[0]
        for r in range(1, k):
            den = den + w[r]
        rows_all.append(jnp.concatenate(e, axis=0) * ROW_WORDS)
        g_all.append(jnp.concatenate(w, axis=0) / den)
    row_ref[...] = jnp.concatenate(rows_all, axis=0).T
    g_ref[...] = jnp.concatenate(g_all, axis=0).T


def _topk_call(sc, tt):
    t = sc.shape[2]
    blk = pl.BlockSpec((tt, PEER_SLOTS), lambda i: (i, 0))
    return pl.pallas_call(
        _topk_kernel,
        grid=(t // tt,),
        in_specs=[pl.BlockSpec((2 * PEER_HEADS, PEER_NKEYS, tt), lambda i: (0, 0, i))],
        out_specs=[blk, blk],
        out_shape=[jax.ShapeDtypeStruct((t, PEER_SLOTS), I32),
                   jax.ShapeDtypeStruct((t, PEER_SLOTS), F32)],
        compiler_params=pltpu.CompilerParams(dimension_semantics=("arbitrary",), vmem_limit_bytes=VMEM_LIMIT),
        name="topk",
    )(sc)


def _chunk_operand(tile_ref, c):
    return pltpu.bitcast(tile_ref[pl.ds(c, PEER_SLOTS, stride=ROW_WORDS), :], BF16)


def _gather_group(idx_ref, slot, tab_ref, tiles):
    for s in range(PEER_GROUP):
        for j in range(PEER_SLOTS):
            r = pl.multiple_of(idx_ref[slot, s, j], ROW_WORDS)
            tiles[s][pl.ds(j * ROW_WORDS, ROW_WORDS), :] = tab_ref[pl.ds(r, ROW_WORDS), :]


def _pipelined_groups(rows_hbm, idx_ref, sem, tab_ref, tiles, n_groups, compute):
    first = pl.program_id(0) * n_groups

    def idx_copy(g, slot):
        src = rows_hbm.at[first + jnp.minimum(g, n_groups - 1)]
        return pltpu.make_async_copy(src, idx_ref.at[slot], sem.at[slot])

    for g in range(IDX_SLOTS - 1):
        idx_copy(g, g).start()
    idx_copy(0, 0).wait()
    _gather_group(idx_ref, 0, tab_ref, tiles)
    idx_copy(IDX_SLOTS - 1, IDX_SLOTS - 1).start()

    def body(it, carry):
        for k in range(IDX_SLOTS):
            g = IDX_SLOTS * it + k
            nxt = (k + 1) % IDX_SLOTS
            idx_copy(g + 1, nxt).wait()
            compute(g)
            _gather_group(idx_ref, nxt, tab_ref, tiles)
            idx_copy(g + IDX_SLOTS, k).start()
        return carry

    lax.fori_loop(0, n_groups // IDX_SLOTS, body, 0)
    for slot in range(1, IDX_SLOTS):
        idx_copy(n_groups, slot).wait()


def _group_rows(g):
    t0 = g * PEER_GROUP
    base = pl.multiple_of((t0 // PACK_ROWS) * PACK_ROWS, PACK_ROWS)
    return base, t0 - base


def _peer_u_kernel(rows_hbm, h2_ref, g_ref, tab_ref, coef_ref, idx_ref, sem, *scratch):
    tiles, acc_lo, acc_hi = scratch[:PEER_GROUP], scratch[PEER_GROUP], scratch[PEER_GROUP + 1]
    tt = h2_ref.shape[0]
    width = 2 * PEER_SLOTS
    sub = lax.broadcasted_iota(I32, (PACK_ROWS, width), 0)
    acc_lo[...] = jnp.zeros(acc_lo.shape, F32)
    acc_hi[...] = jnp.zeros(acc_hi.shape, F32)

    def compute(g):
        base, r0 = _group_rows(g)
        xg = h2_ref[pl.ds(base, PACK_ROWS), :]
        lhs = [jnp.concatenate([xg[:, c * LANES:(c + 1) * LANES],
                                xg[:, D_HALF + c * LANES:D_HALF + (c + 1) * LANES]], axis=0)
               for c in range(ROW_WORDS)]
        lo = acc_lo[pl.ds(base, PACK_ROWS), :]
        hi = acc_hi[pl.ds(base, PACK_ROWS), :]
        for s in range(PEER_GROUP):
            res = _dot_t(lhs[0], _chunk_operand(tiles[s], 0))
            for c in range(1, ROW_WORDS):
                res = res + _dot_t(lhs[c], _chunk_operand(tiles[s], c))
            lo = jnp.where(sub == r0 + s, res[:PACK_ROWS], lo)
            hi = jnp.where(sub == r0 + s, res[PACK_ROWS:], hi)
        acc_lo[pl.ds(base, PACK_ROWS), :] = lo
        acc_hi[pl.ds(base, PACK_ROWS), :] = hi

    _pipelined_groups(rows_hbm, idx_ref, sem, tab_ref, tiles, tt // PEER_GROUP, compute)
    col = lax.broadcasted_iota(I32, (tt, width), 1)
    halves = jnp.where(col % 2 == 0, acc_lo[...], acc_hi[...])
    act = _gelu(halves + pltpu.roll(halves, width - 1, 1))
    src = lax.broadcasted_iota(I32, (PEER_SLOTS, width), 0)
    dst = lax.broadcasted_iota(I32, (PEER_SLOTS, width), 1)
    spread = jnp.where(dst == 2 * src, 1.0, 0.0).astype(BF16)
    coef = act * _dot(g_ref[...].astype(BF16), spread)
    coef_ref[:, :width] = coef.astype(BF16)
    coef_ref[:, width:] = pltpu.roll(coef, 1, 1).astype(BF16)


def _gather_scratch():
    return ([pltpu.SMEM((IDX_SLOTS, PEER_GROUP, PEER_SLOTS), I32), pltpu.SemaphoreType.DMA((IDX_SLOTS,))]
            + [pltpu.VMEM((ROW_WORDS * PEER_SLOTS, LANES), I32) for _ in range(PEER_GROUP)])


def _peer_u_call(rows, h2, g, u_tab, tt):
    t = h2.shape[0]
    tok = lambda i: (i, 0)
    width = 2 * PEER_SLOTS
    return pl.pallas_call(
        _peer_u_kernel,
        grid=(t // tt,),
        in_specs=[
            pl.BlockSpec(memory_space=pl.ANY),
            pl.BlockSpec((tt, D_MODEL), tok),
            pl.BlockSpec((tt, PEER_SLOTS), tok),
            pl.BlockSpec(u_tab.shape, lambda i: (0, 0), pipeline_mode=pl.Buffered(1)),
        ],
        out_specs=pl.BlockSpec((tt, 2 * width), tok),
        out_shape=jax.ShapeDtypeStruct((t, 2 * width), BF16),
        scratch_shapes=_gather_scratch() + [pltpu.VMEM((tt, width), F32)] * 2,
        compiler_params=pltpu.CompilerParams(dimension_semantics=("arbitrary",),
                                             vmem_limit_bytes=VMEM_LIMIT_TABLE),
        name="peer_u",
    )(rows, h2, g, u_tab)


def _peer_v_kernel(rows_hbm, coef_ref, x1_ref, gf_ref, tab_ref, out_ref, idx_ref, sem, *scratch):
    tiles, acc_ref = scratch[:PEER_GROUP], scratch[PEER_GROUP]
    tt = coef_ref.shape[0]
    width = 2 * PEER_SLOTS
    sub = lax.broadcasted_iota(I32, (PACK_ROWS, LANES), 0)
    acc_ref[...] = jnp.zeros(acc_ref.shape, F32)

    def compute(g):
        base, r0 = _group_rows(g)
        cg = jnp.concatenate([coef_ref[pl.ds(base, PACK_ROWS), :width],
                              coef_ref[pl.ds(base, PACK_ROWS), width:]], axis=0)
        cols = [h * D_HALF + c * LANES for c in range(ROW_WORDS) for h in range(2)]
        outs = [acc_ref[pl.ds(base, PACK_ROWS), o:o + LANES] for o in cols]
        for s in range(PEER_GROUP):
            for c in range(ROW_WORDS):
                res = _dot(cg, _chunk_operand(tiles[s], c))
                outs[2 * c] = jnp.where(sub == r0 + s, res[:PACK_ROWS], outs[2 * c])
                outs[2 * c + 1] = jnp.where(sub == r0 + s, res[PACK_ROWS:], outs[2 * c + 1])
        for o, val in zip(cols, outs):
            acc_ref[pl.ds(base, PACK_ROWS), o:o + LANES] = val

    _pipelined_groups(rows_hbm, idx_ref, sem, tab_ref, tiles, tt // PEER_GROUP, compute)
    out_ref[...] = _rms(x1_ref[...] + acc_ref[...], gf_ref[...])


def _peer_v_call(rows, coef, x1, gf, v_tab, tt):
    t = x1.shape[0]
    tok = lambda i: (i, 0)
    return pl.pallas_call(
        _peer_v_kernel,
        grid=(t // tt,),
        in_specs=[
            pl.BlockSpec(memory_space=pl.ANY),
            pl.BlockSpec((tt, 4 * PEER_SLOTS), tok),
            pl.BlockSpec((tt, D_MODEL), tok),
            pl.BlockSpec((1, D_MODEL), lambda i: (0, 0)),
            pl.BlockSpec(v_tab.shape, lambda i: (0, 0), pipeline_mode=pl.Buffered(1)),
        ],
        out_specs=pl.BlockSpec((tt, D_MODEL), tok),
        out_shape=jax.ShapeDtypeStruct((t, D_MODEL), F32),
        scratch_shapes=_gather_scratch() + [pltpu.VMEM((tt, D_MODEL), F32)],
        compiler_params=pltpu.CompilerParams(dimension_semantics=("arbitrary",),
                                             vmem_limit_bytes=VMEM_LIMIT_TABLE),
        name="peer_v",
    )(rows, coef, x1, gf, v_tab)


def _pack_rows(tab):
    bits = lax.bitcast_convert_type(tab.astype(BF16), jnp.uint16).astype(jnp.uint32)
    word = bits[:, :D_HALF] | (bits[:, D_HALF:] << 16)
    return lax.bitcast_convert_type(word, I32).reshape(tab.shape[0] * ROW_WORDS, LANES)


def _rope_tables(seq):
    half = DA_HEAD_DIM // 2
    inv = 1.0 / (ROPE_THETA ** (jnp.arange(0, DA_HEAD_DIM, 2, dtype=F32) / DA_HEAD_DIM))
    ang = jnp.arange(seq, dtype=F32)[:, None] * inv[None, :]
    cos, sin = jnp.cos(ang), jnp.sin(ang)
    reps = LANES // DA_HEAD_DIM
    cos_t = jnp.tile(jnp.concatenate([cos, cos], axis=1), (1, reps))
    sin_t = jnp.tile(jnp.concatenate([-sin, sin], axis=1), (1, reps))
    del half
    return cos_t, sin_t


def kernel(x, norm1_g, w_in, lambda_q1, lambda_k1, lambda_q2, lambda_k2, subln_g, sgu_ln_g, sgu_ln_b, w_spatial, b_spatial, w_branch_a, w_branch_b, w_gate, b_gate, w_out, norm2_g, peer_wq, peer_subkeys, peer_u, peer_v, norm_f_g):
    b, s, d = x.shape
    t = b * s
    l = 0
    x2 = x.reshape(t, d)
    cos_t, sin_t = _rope_tables(s)
    w_cat = jnp.concatenate([w_in[l], w_gate[l]], axis=1).astype(BF16)

    tq = min(ATTN_TQ, s)
    q, k0, k1, vt, u_act, sv_ln, gates = _proj_call(
        x2, norm1_g[l][None], w_cat, b_gate[l][None], cos_t, sin_t, sgu_ln_g[l][None], sgu_ln_b[l][None],
        seq=s, tm=tq)

    lam4 = jnp.stack([lambda_q1[l], lambda_k1[l], lambda_q2[l], lambda_k2[l]]).astype(F32)
    ya = _attn_call(lam4, q, k0, k1, vt, subln_g[l][None], batch=b, tq=tq)

    b_s_full = jnp.repeat(b_spatial[l].T, SG_GROUP_DIM, axis=1)
    sk = peer_subkeys[l].reshape(2 * PEER_HEADS, PEER_NKEYS, PEER_HALF).astype(BF16)
    x1, h2, sc = _mix_call(x2, u_act, sv_ln, gates, ya, w_spatial[l], b_s_full,
                           w_branch_a[l].astype(BF16), w_branch_b[l].astype(BF16), w_out[l].astype(BF16),
                           norm2_g[l][None], peer_wq[l].astype(BF16), sk, tm=min(MIX_TM, t))

    rows, g = _topk_call(sc, tt=min(TOPK_TT, t))
    rows = rows.reshape(t // PEER_GROUP, PEER_GROUP, PEER_SLOTS)
    coef = _peer_u_call(rows, h2, g, _pack_rows(peer_u[l]), tt=min(PEER_TT, t))
    out = _peer_v_call(rows, coef, x1, norm_f_g[None], _pack_rows(peer_v[l]), tt=min(PEER_TT, t))
    return out.reshape(b, s, d)
```

```python
import functools
import math

import jax
import jax.numpy as jnp
from jax import lax
from jax.experimental import pallas as pl
from jax.experimental.pallas import tpu as pltpu

F32 = jnp.float32
BF16 = jnp.bfloat16
I32 = jnp.int32

D_MODEL = 1024
CHUNK = 64
ROPE_THETA = 10000.0
EPS = 1e-6

DA_HEADS = 4
DA_HEAD_DIM = 64
DA_V_DIM = 2 * DA_HEAD_DIM
DA_WIDTH = DA_HEADS * DA_V_DIM

SG_GROUPS = 4
SG_BLOCK = 128
SG_GROUP_DIM = 128
SG_WIDTH = SG_GROUPS * SG_GROUP_DIM

PEER_HEADS = 8
PEER_NKEYS = 128
PEER_HALF = 128
PEER_TOPK = 16
PEER_SLOTS = PEER_HEADS * PEER_TOPK
PEER_N_EXPERTS = PEER_NKEYS * PEER_NKEYS

LANES = 128
SUBLANES = 8
PACK_ROWS = 16
PEER_GROUP = 8
D_HALF = D_MODEL // 2
ROW_WORDS = D_HALF // LANES
IDX_SLOTS = 4

ATTN_TQ = 1024
MIX_TM = 256
TOPK_TT = 256
PEER_TT = 512

VMEM_LIMIT = 48 * 1024 * 1024
VMEM_LIMIT_TABLE = 56 * 1024 * 1024

LAMBDA_INIT = 0.8 - 0.6 * math.exp(-0.3 * 0)
LOG2E = math.log2(math.e)


def _gelu(x):
    return 0.5 * x * (1.0 + lax.erf(x * (0.5 ** 0.5)))


def _rms(x, g):
    return x * lax.rsqrt(jnp.mean(x * x, axis=-1, keepdims=True) + EPS) * g


def _dot(a, b):
    return jnp.dot(a, b, preferred_element_type=F32)


def _dot_t(a, b):
    return lax.dot_general(a, b, (((1,), (1,)), ((), ())), preferred_element_type=F32)


def _proj_kernel(x_ref, g1_ref, w_ref, bg_ref, cos_ref, sin_ref, lng_ref, lnb_ref,
                 qt_ref, k0_ref, k1_ref, vt_ref, u_ref, sv_ref, gate_ref):
    tm = x_ref.shape[0]
    hb = _rms(x_ref[...], g1_ref[...]).astype(BF16)
    cos = cos_ref[...]
    sin = sin_ref[...]
    lane = lax.broadcasted_iota(I32, (tm, LANES), 1)
    first_half = (lane % DA_HEAD_DIM) < (DA_HEAD_DIM // 2)

    def rope(t):
        outs = []
        for b in range(DA_WIDTH // LANES):
            blk = t[:, b * LANES:(b + 1) * LANES]
            swapped = jnp.where(first_half, pltpu.roll(blk, LANES - DA_HEAD_DIM // 2, 1),
                                pltpu.roll(blk, DA_HEAD_DIM // 2, 1))
            outs.append(blk * cos + swapped * sin)
        return jnp.concatenate(outs, axis=1)

    o = 0
    q = _dot(hb, w_ref[:, o:o + DA_WIDTH]); o += DA_WIDTH
    qt_ref[0] = (rope(q) * (DA_HEAD_DIM ** -0.5 * LOG2E)).T.astype(BF16)
    k = rope(_dot(hb, w_ref[:, o:o + DA_WIDTH])); o += DA_WIDTH
    comp0 = (lax.broadcasted_iota(I32, k.shape, 1) % DA_V_DIM) < DA_HEAD_DIM
    k0_ref[...] = jnp.where(comp0, k, 0.0).astype(BF16)
    k1_ref[...] = jnp.where(comp0, 0.0, k).astype(BF16)
    vt_ref[0] = _dot(hb, w_ref[:, o:o + DA_WIDTH]).T.astype(BF16); o += DA_WIDTH
    u_ref[...] = _gelu(_dot(hb, w_ref[:, o:o + SG_WIDTH])).astype(BF16); o += SG_WIDTH
    sv = _gelu(_dot(hb, w_ref[:, o:o + SG_WIDTH])); o += SG_WIDTH
    mu = jnp.mean(sv, axis=-1, keepdims=True)
    svc = sv - mu
    sv_ref[...] = (svc * lax.rsqrt(jnp.mean(svc * svc, axis=-1, keepdims=True) + EPS) * lng_ref[...]
                   + lnb_ref[...]).astype(BF16)
    gate_ref[...] = jax.nn.sigmoid(_dot(hb, w_ref[:, o:o + 2 * D_MODEL]) + bg_ref[...]).astype(BF16)


def _proj_call(x2, g1, w_cat, b_gate, cos_t, sin_t, ln_g, ln_b, seq, tm):
    t = x2.shape[0]
    nseq = seq // tm
    row = lambda i: (i, 0)
    fixed = lambda i: (0, 0)
    pos = lambda i: (i % nseq, 0)
    wcols = w_cat.shape[1]
    transposed = jax.ShapeDtypeStruct((t // tm, DA_WIDTH, tm), BF16)
    tblk = pl.BlockSpec((1, DA_WIDTH, tm), lambda i: (i, 0, 0))
    out_shapes = [transposed] + [jax.ShapeDtypeStruct((t, DA_WIDTH), BF16)] * 2 + [transposed] + \
                 [jax.ShapeDtypeStruct((t, SG_WIDTH), BF16)] * 2 + \
                 [jax.ShapeDtypeStruct((t, 2 * D_MODEL), BF16)]
    return pl.pallas_call(
        _proj_kernel,
        grid=(t // tm,),
        in_specs=[
            pl.BlockSpec((tm, D_MODEL), row),
            pl.BlockSpec((1, D_MODEL), fixed),
            pl.BlockSpec((D_MODEL, wcols), fixed, pipeline_mode=pl.Buffered(1)),
            pl.BlockSpec((1, 2 * D_MODEL), fixed),
            pl.BlockSpec((tm, LANES), pos),
            pl.BlockSpec((tm, LANES), pos),
            pl.BlockSpec((1, SG_WIDTH), fixed),
            pl.BlockSpec((1, SG_WIDTH), fixed),
        ],
        out_specs=[tblk] + [pl.BlockSpec((tm, DA_WIDTH), row)] * 2 + [tblk] +
                  [pl.BlockSpec((tm, SG_WIDTH), row)] * 2 +
                  [pl.BlockSpec((tm, 2 * D_MODEL), row)],
        out_shape=out_shapes,
        compiler_params=pltpu.CompilerParams(dimension_semantics=("arbitrary",), vmem_limit_bytes=VMEM_LIMIT),
        name="proj",
    )(x2, g1, w_cat, b_gate, cos_t, sin_t, ln_g, ln_b)


def _attn_kernel(lam_ref, qt_ref, k0_ref, k1_ref, vt_ref, sg_ref, o_ref, acc_ref, st_ref, m_ref, mprev_ref, l_ref,
                 *, tq):
    i = pl.program_id(2)
    qt = qt_ref[0]

    m_ref[...] = jnp.full(m_ref.shape, -1e30, F32)
    l_ref[...] = jnp.zeros(l_ref.shape, F32)
    acc_ref[...] = jnp.zeros(acc_ref.shape, F32)

    def scores(j, masked):
        start = pl.multiple_of(j * tq, tq)
        if masked:
            keyc = lax.broadcasted_iota(I32, (tq, tq), 0) // CHUNK
            qryc = lax.broadcasted_iota(I32, (tq, tq), 1) // CHUNK
            visible = keyc <= qryc
        for c, k_ref in enumerate((k0_ref, k1_ref)):
            st = _dot(k_ref[pl.ds(start, tq), :], qt)
            if masked:
                st = jnp.where(visible, st, -jnp.inf)
            st_ref[c] = st
            m_old = m_ref[c]
            mprev_ref[c] = m_old
            m_ref[c] = jnp.maximum(m_old, jnp.max(st, axis=0, keepdims=True))

    def accumulate(j):
        vt = vt_ref[j]
        for c in range(2):
            m_new = m_ref[c]
            alpha = jnp.exp2(mprev_ref[c] - m_new)
            p = jnp.exp2(st_ref[c] - m_new)
            l_ref[c] = alpha * l_ref[c] + jnp.sum(p, axis=0, keepdims=True)
            acc_ref[c] = alpha * acc_ref[c] + _dot(vt, p.astype(BF16))

    scores(i, True)

    def body(j, prev):
        accumulate(prev)
        scores(j, False)
        return j

    accumulate(lax.fori_loop(0, i, body, i))

    lam = (jnp.exp(jnp.sum(lam_ref[0:1, :] * lam_ref[1:2, :], axis=-1, keepdims=True))
           - jnp.exp(jnp.sum(lam_ref[2:3, :] * lam_ref[3:4, :], axis=-1, keepdims=True)) + LAMBDA_INIT)
    o = acc_ref[0] / l_ref[0] - lam * (acc_ref[1] / l_ref[1])
    o = o * lax.rsqrt(jnp.mean(o * o, axis=0, keepdims=True) + EPS)
    o_ref[...] = (o.T * sg_ref[...] * (1.0 - LAMBDA_INIT)).astype(BF16)


def _attn_call(lam4, qt, k0, k1, vt, subln_g, batch, tq):
    t = k0.shape[0]
    s = t // batch
    nq = s // tq
    kern = functools.partial(_attn_kernel, tq=tq)
    kblk = pl.BlockSpec((s, DA_V_DIM), lambda bi, h, i: (bi, h))
    stat = pltpu.VMEM((2, 1, tq), F32)
    return pl.pallas_call(
        kern,
        grid=(batch, DA_HEADS, nq),
        in_specs=[
            pl.BlockSpec((4, DA_HEAD_DIM), lambda bi, h, i: (0, 0)),
            pl.BlockSpec((1, DA_V_DIM, tq), lambda bi, h, i: (bi * nq + i, h, 0)),
            kblk, kblk,
            pl.BlockSpec((nq, DA_V_DIM, tq), lambda bi, h, i: (bi, h, 0)),
            pl.BlockSpec((1, DA_V_DIM), lambda bi, h, i: (0, 0)),
        ],
        out_specs=pl.BlockSpec((tq, DA_V_DIM), lambda bi, h, i: (bi * nq + i, h)),
        out_shape=jax.ShapeDtypeStruct((t, DA_WIDTH), BF16),
        scratch_shapes=[
            pltpu.VMEM((2, DA_V_DIM, tq), F32),
            pltpu.VMEM((2, tq, tq), F32),
            stat, stat, stat,
        ],
        compiler_params=pltpu.CompilerParams(dimension_semantics=("arbitrary",) * 3, vmem_limit_bytes=VMEM_LIMIT),
        name="attn",
    )(lam4, qt, k0, k1, vt, subln_g)


def _mix_kernel(x_ref, u_ref, sv_ref, gate_ref, ya_ref, ws_ref, bs_ref, wa_ref, wb_ref, wo_ref, g2_ref,
                wq_ref, sk_ref, x1_ref, h2_ref, sc_ref):
    tm = x_ref.shape[0]
    pc = lax.broadcasted_iota(I32, (SG_BLOCK, SG_BLOCK), 0) // CHUNK
    pr = lax.broadcasted_iota(I32, (SG_BLOCK, SG_BLOCK), 1) // CHUNK
    causal = pr <= pc
    rows = []
    for n in range(tm // SG_BLOCK):
        cols = []
        for g in range(SG_GROUPS):
            w = jnp.where(causal, ws_ref[g], 0.0).astype(BF16)
            vb = sv_ref[n * SG_BLOCK:(n + 1) * SG_BLOCK, g * SG_GROUP_DIM:(g + 1) * SG_GROUP_DIM]
            cols.append(_dot(w, vb))
        rows.append(jnp.concatenate(cols, axis=1) + bs_ref[...])
    s = jnp.concatenate(rows, axis=0)
    yb = (u_ref[...].astype(F32) * s).astype(BF16)
    merged = (gate_ref[:, :D_MODEL].astype(F32) * _dot(ya_ref[...], wa_ref[...])
              + gate_ref[:, D_MODEL:].astype(F32) * _dot(yb, wb_ref[...]))
    x1 = x_ref[...] + _dot(merged.astype(BF16), wo_ref[...])
    x1_ref[...] = x1
    h2 = _rms(x1, g2_ref[...]).astype(BF16)
    h2_ref[...] = h2
    qp = _dot(h2, wq_ref[...]).astype(BF16)
    for hp in range(2 * PEER_HEADS):
        sc_ref[hp] = _dot_t(sk_ref[hp], qp[:, hp * PEER_HALF:(hp + 1) * PEER_HALF])


def _mix_call(x2, u_act, sv_ln, gates, ya, w_s, b_s_full, w_a, w_b, w_o, g2, w_q, sk, tm):
    t = x2.shape[0]
    row = lambda i: (i, 0)
    fixed2 = lambda i: (0, 0)
    fixed3 = lambda i: (0, 0, 0)
    return pl.pallas_call(
        _mix_kernel,
        grid=(t // tm,),
        in_specs=[
            pl.BlockSpec((tm, D_MODEL), row),
            pl.BlockSpec((tm, SG_WIDTH), row),
            pl.BlockSpec((tm, SG_WIDTH), row),
            pl.BlockSpec((tm, 2 * D_MODEL), row),
            pl.BlockSpec((tm, DA_WIDTH), row),
            pl.BlockSpec((SG_GROUPS, SG_BLOCK, SG_BLOCK), fixed3),
            pl.BlockSpec((SG_BLOCK, SG_WIDTH), fixed2),
            pl.BlockSpec((DA_WIDTH, D_MODEL), fixed2),
            pl.BlockSpec((SG_WIDTH, D_MODEL), fixed2),
            pl.BlockSpec((D_MODEL, D_MODEL), fixed2),
            pl.BlockSpec((1, D_MODEL), fixed2),
            pl.BlockSpec((D_MODEL, 2 * PEER_HEADS * PEER_HALF), fixed2),
            pl.BlockSpec((2 * PEER_HEADS, PEER_NKEYS, PEER_HALF), fixed3),
        ],
        out_specs=[
            pl.BlockSpec((tm, D_MODEL), row),
            pl.BlockSpec((tm, D_MODEL), row),
            pl.BlockSpec((2 * PEER_HEADS, PEER_NKEYS, tm), lambda i: (0, 0, i)),
        ],
        out_shape=[
            jax.ShapeDtypeStruct((t, D_MODEL), F32),
            jax.ShapeDtypeStruct((t, D_MODEL), BF16),
            jax.ShapeDtypeStruct((2 * PEER_HEADS, PEER_NKEYS, t), F32),
        ],
        compiler_params=pltpu.CompilerParams(dimension_semantics=("arbitrary",), vmem_limit_bytes=VMEM_LIMIT),
        name="mix",
    )(x2, u_act, sv_ln, gates, ya, w_s, b_s_full, w_a, w_b, w_o, g2, w_q, sk)


def _top16(vals, n):
    tt = vals.shape[1]
    pos = lax.broadcasted_iota(I32, (n, tt), 0)
    nblk = n // SUBLANES
    row = lax.broadcasted_iota(I32, (SUBLANES, tt), 0)
    pos_blocks = [row + b * SUBLANES for b in range(nblk)]
    cur = vals
    out_v, out_i = [], []
    for _ in range(PEER_TOPK):
        level = [(cur[b * SUBLANES:(b + 1) * SUBLANES], pos_blocks[b]) for b in range(nblk)]
        while len(level) > 1:
            nxt = []
            for a in range(0, len(level) - 1, 2):
                (va, ia), (vb, ib) = level[a], level[a + 1]
                take_b = vb > va
                nxt.append((jnp.where(take_b, vb, va), jnp.where(take_b, ib, ia)))
            if len(level) % 2:
                nxt.append(level[-1])
            level = nxt
        v8, i8 = level[0]
        m = jnp.max(v8, axis=0, keepdims=True)
        idx = jnp.min(jnp.where(v8 == m, i8, n), axis=0, keepdims=True)
        out_v.append(m)
        out_i.append(idx)
        cur = jnp.where(pos == idx, -jnp.inf, cur)
    return out_v, out_i


def _topk_kernel(sc_ref, row_ref, g_ref):
    tt = sc_ref.shape[2]
    k = PEER_TOPK
    rows_all, g_all = [], []
    for h in range(PEER_HEADS):
        s1, i1 = _top16(sc_ref[2 * h], PEER_NKEYS)
        s2, i2 = _top16(sc_ref[2 * h + 1], PEER_NKEYS)
        s2m = jnp.concatenate(s2, axis=0)
        i2m = jnp.concatenate(i2, axis=0)
        keep = [k // (a + 1) for a in range(k)]
        n_cand = -(-sum(keep) // SUBLANES) * SUBLANES
        pad = n_cand - sum(keep)
        cand = jnp.concatenate([s1[a] + s2m[:keep[a]] for a in range(k)]
                               + [jnp.full((pad, tt), -jnp.inf, F32)], axis=0)
        cidx = jnp.concatenate([i1[a] * PEER_NKEYS + i2m[:keep[a]] for a in range(k)]
                               + [jnp.zeros((pad, tt), I32)], axis=0)
        top_s, pos = _top16(cand, n_cand)
        cpos = lax.broadcasted_iota(I32, (n_cand, tt), 0)
        e = [jnp.sum(jnp.where(cpos == pos[r], cidx, 0), axis=0, keepdims=True) for r in range(k)]
        w = [jnp.exp(top_s[r] - top_s[0]) for r in range(k)]
        den = w[0]
        for r in range(1, k):
            den = den + w[r]
        rows_all.append(jnp.concatenate(e, axis=0) * ROW_WORDS)
        g_all.append(jnp.concatenate(w, axis=0) / den)
    row_ref[...] = jnp.concatenate(rows_all, axis=0).T
    g_ref[...] = jnp.concatenate(g_all, axis=0).T


def _topk_call(sc, tt):
    t = sc.shape[2]
    blk = pl.BlockSpec((tt, PEER_SLOTS), lambda i: (i, 0))
    return pl.pallas_call(
        _topk_kernel,
        grid=(t // tt,),
        in_specs=[pl.BlockSpec((2 * PEER_HEADS, PEER_NKEYS, tt), lambda i: (0, 0, i))],
        out_specs=[blk, blk],
        out_shape=[jax.ShapeDtypeStruct((t, PEER_SLOTS), I32),
                   jax.ShapeDtypeStruct((t, PEER_SLOTS), F32)],
        compiler_params=pltpu.CompilerParams(dimension_semantics=("arbitrary",), vmem_limit_bytes=VMEM_LIMIT),
        name="topk",
    )(sc)


def _chunk_operand(tile_ref, c):
    return pltpu.bitcast(tile_ref[pl.ds(c, PEER_SLOTS, stride=ROW_WORDS), :], BF16)


def _gather_group(idx_ref, slot, tab_ref, tiles):
    for s in range(PEER_GROUP):
        for j in range(PEER_SLOTS):
            r = pl.multiple_of(idx_ref[slot, s, j], ROW_WORDS)
            tiles[s][pl.ds(j * ROW_WORDS, ROW_WORDS), :] = tab_ref[pl.ds(r, ROW_WORDS), :]


def _pipelined_groups(rows_hbm, idx_ref, sem, tab_ref, tiles, n_groups, compute):
    first = pl.program_id(0) * n_groups

    def idx_copy(g, slot):
        src = rows_hbm.at[first + jnp.minimum(g, n_groups - 1)]
        return pltpu.make_async_copy(src, idx_ref.at[slot], sem.at[slot])

    for g in range(IDX_SLOTS - 1):
        idx_copy(g, g).start()
    idx_copy(0, 0).wait()
    _gather_group(idx_ref, 0, tab_ref, tiles)
    idx_copy(IDX_SLOTS - 1, IDX_SLOTS - 1).start()

    def body(it, carry):
        for k in range(IDX_SLOTS):
            g = IDX_SLOTS * it + k
            nxt = (k + 1) % IDX_SLOTS
            idx_copy(g + 1, nxt).wait()
            compute(g)
            _gather_group(idx_ref, nxt, tab_ref, tiles)
            idx_copy(g + IDX_SLOTS, k).start()
        return carry

    lax.fori_loop(0, n_groups // IDX_SLOTS, body, 0)
    for slot in range(1, IDX_SLOTS):
        idx_copy(n_groups, slot).wait()


def _group_rows(g):
    t0 = g * PEER_GROUP
    base = pl.multiple_of((t0 // PACK_ROWS) * PACK_ROWS, PACK_ROWS)
    return base, t0 - base


def _peer_u_kernel(rows_hbm, h2_ref, g_ref, tab_ref, coef_ref, idx_ref, sem, *scratch):
    tiles, acc_lo, acc_hi = scratch[:PEER_GROUP], scratch[PEER_GROUP], scratch[PEER_GROUP + 1]
    tt = h2_ref.shape[0]
    width = 2 * PEER_SLOTS
    sub = lax.broadcasted_iota(I32, (PACK_ROWS, width), 0)
    acc_lo[...] = jnp.zeros(acc_lo.shape, F32)
    acc_hi[...] = jnp.zeros(acc_hi.shape, F32)

    def compute(g):
        base, r0 = _group_rows(g)
        xg = h2_ref[pl.ds(base, PACK_ROWS), :]
        lhs = [jnp.concatenate([xg[:, c * LANES:(c + 1) * LANES],
                                xg[:, D_HALF + c * LANES:D_HALF + (c + 1) * LANES]], axis=0)
               for c in range(ROW_WORDS)]
        lo = acc_lo[pl.ds(base, PACK_ROWS), :]
        hi = acc_hi[pl.ds(base, PACK_ROWS), :]
        for s in range(PEER_GROUP):
            res = _dot_t(lhs[0], _chunk_operand(tiles[s], 0))
            for c in range(1, ROW_WORDS):
                res = res + _dot_t(lhs[c], _chunk_operand(tiles[s], c))
            lo = jnp.where(sub == r0 + s, res[:PACK_ROWS], lo)
            hi = jnp.where(sub == r0 + s, res[PACK_ROWS:], hi)
        acc_lo[pl.ds(base, PACK_ROWS), :] = lo
        acc_hi[pl.ds(base, PACK_ROWS), :] = hi

    _pipelined_groups(rows_hbm, idx_ref, sem, tab_ref, tiles, tt // PEER_GROUP, compute)
    col = lax.broadcasted_iota(I32, (tt, width), 1)
    halves = jnp.where(col % 2 == 0, acc_lo[...], acc_hi[...])
    act = _gelu(halves + pltpu.roll(halves, width - 1, 1))
    src = lax.broadcasted_iota(I32, (PEER_SLOTS, width), 0)
    dst = lax.broadcasted_iota(I32, (PEER_SLOTS, width), 1)
    spread = jnp.where(dst == 2 * src, 1.0, 0.0).astype(BF16)
    coef = act * _dot(g_ref[...].astype(BF16), spread)
    coef_ref[:, :width] = coef.astype(BF16)
    coef_ref[:, width:] = pltpu.roll(coef, 1, 1).astype(BF16)


def _gather_scratch():
    return ([pltpu.SMEM((IDX_SLOTS, PEER_GROUP, PEER_SLOTS), I32), pltpu.SemaphoreType.DMA((IDX_SLOTS,))]
            + [pltpu.VMEM((ROW_WORDS * PEER_SLOTS, LANES), I32) for _ in range(PEER_GROUP)])


def _peer_u_call(rows, h2, g, u_tab, tt):
    t = h2.shape[0]
    tok = lambda i: (i, 0)
    width = 2 * PEER_SLOTS
    return pl.pallas_call(
        _peer_u_kernel,
        grid=(t // tt,),
        in_specs=[
            pl.BlockSpec(memory_space=pl.ANY),
            pl.BlockSpec((tt, D_MODEL), tok),
            pl.BlockSpec((tt, PEER_SLOTS), tok),
            pl.BlockSpec(u_tab.shape, lambda i: (0, 0), pipeline_mode=pl.Buffered(1)),
        ],
        out_specs=pl.BlockSpec((tt, 2 * width), tok),
        out_shape=jax.ShapeDtypeStruct((t, 2 * width), BF16),
        scratch_shapes=_gather_scratch() + [pltpu.VMEM((tt, width), F32)] * 2,
        compiler_params=pltpu.CompilerParams(dimension_semantics=("arbitrary",),
                                             vmem_limit_bytes=VMEM_LIMIT_TABLE),
        name="peer_u",
    )(rows, h2, g, u_tab)


def _peer_v_kernel(rows_hbm, coef_ref, x1_ref, gf_ref, tab_ref, out_ref, idx_ref, sem, *scratch):
    tiles, acc_ref = scratch[:PEER_GROUP], scratch[PEER_GROUP]
    tt = coef_ref.shape[0]
    width = 2 * PEER_SLOTS
    sub = lax.broadcasted_iota(I32, (PACK_ROWS, LANES), 0)
    acc_ref[...] = jnp.zeros(acc_ref.shape, F32)

    def compute(g):
        base, r0 = _group_rows(g)
        cg = jnp.concatenate([coef_ref[pl.ds(base, PACK_ROWS), :width],
                              coef_ref[pl.ds(base, PACK_ROWS), width:]], axis=0)
        cols = [h * D_HALF + c * LANES for c in range(ROW_WORDS) for h in range(2)]
        outs = [acc_ref[pl.ds(base, PACK_ROWS), o:o + LANES] for o in cols]
        for s in range(PEER_GROUP):
            for c in range(ROW_WORDS):
                res = _dot(cg, _chunk_operand(tiles[s], c))
                outs[2 * c] = jnp.where(sub == r0 + s, res[:PACK_ROWS], outs[2 * c])
                outs[2 * c + 1] = jnp.where(sub == r0 + s, res[PACK_ROWS:], outs[2 * c + 1])
        for o, val in zip(cols, outs):
            acc_ref[pl.ds(base, PACK_ROWS), o:o + LANES] = val

    _pipelined_groups(rows_hbm, idx_ref, sem, tab_ref, tiles, tt // PEER_GROUP, compute)
    out_ref[...] = _rms(x1_ref[...] + acc_ref[...], gf_ref[...])


def _peer_v_call(rows, coef, x1, gf, v_tab, tt):
    t = x1.shape[0]
    tok = lambda i: (i, 0)
    return pl.pallas_call(
        _peer_v_kernel,
        grid=(t // tt,),
        in_specs=[
            pl.BlockSpec(memory_space=pl.ANY),
            pl.BlockSpec((tt, 4 * PEER_SLOTS), tok),
            pl.BlockSpec((tt, D_MODEL), tok),
            pl.BlockSpec((1, D_MODEL), lambda i: (0, 0)),
            pl.BlockSpec(v_tab.shape, lambda i: (0, 0), pipeline_mode=pl.Buffered(1)),
        ],
        out_specs=pl.BlockSpec((tt, D_MODEL), tok),
        out_shape=jax.ShapeDtypeStruct((t, D_MODEL), F32),
        scratch_shapes=_gather_scratch() + [pltpu.VMEM((tt, D_MODEL), F32)],
        compiler_params=pltpu.CompilerParams(dimension_semantics=("arbitrary",),
                                             vmem_limit_bytes=VMEM_LIMIT_TABLE),
        name="peer_v",
    )(rows, coef, x1, gf, v_tab)


def _pack_rows(tab):
    bits = lax.bitcast_convert_type(tab.astype(BF16), jnp.uint16).astype(jnp.uint32)
    word = bits[:, :D_HALF] | (bits[:, D_HALF:] << 16)
    return lax.bitcast_convert_type(word, I32).reshape(tab.shape[0] * ROW_WORDS, LANES)


def _rope_tables(seq):
    half = DA_HEAD_DIM // 2
    inv = 1.0 / (ROPE_THETA ** (jnp.arange(0, DA_HEAD_DIM, 2, dtype=F32) / DA_HEAD_DIM))
    ang = jnp.arange(seq, dtype=F32)[:, None] * inv[None, :]
    cos, sin = jnp.cos(ang), jnp.sin(ang)
    reps = LANES // DA_HEAD_DIM
    cos_t = jnp.tile(jnp.concatenate([cos, cos], axis=1), (1, reps))
    sin_t = jnp.tile(jnp.concatenate([-sin, sin], axis=1), (1, reps))
    del half
    return cos_t, sin_t


def kernel(x, norm1_g, w_in, lambda_q1, lambda_k1, lambda_q2, lambda_k2, subln_g, sgu_ln_g, sgu_ln_b, w_spatial, b_spatial, w_branch_a, w_branch_b, w_gate, b_gate, w_out, norm2_g, peer_wq, peer_subkeys, peer_u, peer_v, norm_f_g):
    b, s, d = x.shape
    t = b * s
    l = 0
    x2 = x.reshape(t, d)
    cos_t, sin_t = _rope_tables(s)
    w_cat = jnp.concatenate([w_in[l], w_gate[l]], axis=1).astype(BF16)

    tq = min(ATTN_TQ, s)
    q, k0, k1, vt, u_act, sv_ln, gates = _proj_call(
        x2, norm1_g[l][None], w_cat, b_gate[l][None], cos_t, sin_t, sgu_ln_g[l][None], sgu_ln_b[l][None],
        seq=s, tm=tq)

    lam4 = jnp.stack([lambda_q1[l], lambda_k1[l], lambda_q2[l], lambda_k2[l]]).astype(F32)
    ya = _attn_call(lam4, q, k0, k1, vt, subln_g[l][None], batch=b, tq=tq)

    b_s_full = jnp.repeat(b_spatial[l].T, SG_GROUP_DIM, axis=1)
    sk = peer_subkeys[l].reshape(2 * PEER_HEADS, PEER_NKEYS, PEER_HALF).astype(BF16)
    x1, h2, sc = _mix_call(x2, u_act, sv_ln, gates, ya, w_spatial[l], b_s_full,
                           w_branch_a[l].astype(BF16), w_branch_b[l].astype(BF16), w_out[l].astype(BF16),
                           norm2_g[l][None], peer_wq[l].astype(BF16), sk, tm=min(MIX_TM, t))

    rows, g = _topk_call(sc, tt=min(TOPK_TT, t))
    rows = rows.reshape(t // PEER_GROUP, PEER_GROUP, PEER_SLOTS)
    coef = _peer_u_call(rows, h2, g, _pack_rows(peer_u[l]), tt=min(PEER_TT, t))
    out = _peer_v_call(rows, coef, x1, norm_f_g[None], _pack_rows(peer_v[l]), tt=min(PEER_TT, t))
    return out.reshape(b, s, d)
```

```python
import functools
import math

import jax
import jax.numpy as jnp
from jax import lax
from jax.experimental import pallas as pl
from jax.experimental.pallas import tpu as pltpu

F32 = jnp.float32
BF16 = jnp.bfloat16
I32 = jnp.int32

D_MODEL = 1024
CHUNK = 64
ROPE_THETA = 10000.0
EPS = 1e-6

DA_HEADS = 4
DA_HEAD_DIM = 64
DA_V_DIM = 2 * DA_HEAD_DIM
DA_WIDTH = DA_HEADS * DA_V_DIM

SG_GROUPS = 4
SG_BLOCK = 128
SG_GROUP_DIM = 128
SG_WIDTH = SG_GROUPS * SG_GROUP_DIM

PEER_HEADS = 8
PEER_NKEYS = 128
PEER_HALF = 128
PEER_TOPK = 16
PEER_SLOTS = PEER_HEADS * PEER_TOPK
PEER_N_EXPERTS = PEER_NKEYS * PEER_NKEYS

LANES = 128
SUBLANES = 8
PACK_ROWS = 16
PEER_GROUP = 8
D_HALF = D_MODEL // 2
ROW_WORDS = D_HALF // LANES
IDX_SLOTS = 4

ATTN_TQ = 1024
MIX_TM = 512
TOPK_TT = 256
PEER_TT = 512

VMEM_LIMIT = 48 * 1024 * 1024
VMEM_LIMIT_TABLE = 56 * 1024 * 1024

LAMBDA_INIT = 0.8 - 0.6 * math.exp(-0.3 * 0)
LOG2E = math.log2(math.e)


def _gelu(x):
    return 0.5 * x * (1.0 + lax.erf(x * (0.5 ** 0.5)))


def _rms(x, g):
    return x * lax.rsqrt(jnp.mean(x * x, axis=-1, keepdims=True) + EPS) * g


def _dot(a, b):
    return jnp.dot(a, b, preferred_element_type=F32)


def _dot_t(a, b):
    return lax.dot_general(a, b, (((1,), (1,)), ((), ())), preferred_element_type=F32)


def _proj_kernel(x_ref, g1_ref, w_ref, bg_ref, cos_ref, sin_ref, lng_ref, lnb_ref,
                 qt_ref, k0_ref, k1_ref, vt_ref, u_ref, sv_ref, gate_ref):
    tm = x_ref.shape[0]
    hb = _rms(x_ref[...], g1_ref[...]).astype(BF16)
    cos = cos_ref[...]
    sin = sin_ref[...]
    lane = lax.broadcasted_iota(I32, (tm, LANES), 1)
    first_half = (lane % DA_HEAD_DIM) < (DA_HEAD_DIM // 2)

    def rope(t):
        outs = []
        for b in range(DA_WIDTH // LANES):
            blk = t[:, b * LANES:(b + 1) * LANES]
            swapped = jnp.where(first_half, pltpu.roll(blk, LANES - DA_HEAD_DIM // 2, 1),
                                pltpu.roll(blk, DA_HEAD_DIM // 2, 1))
            outs.append(blk * cos + swapped * sin)
        return jnp.concatenate(outs, axis=1)

    o = 0
    q = _dot(hb, w_ref[:, o:o + DA_WIDTH]); o += DA_WIDTH
    qt_ref[0] = (rope(q) * (DA_HEAD_DIM ** -0.5 * LOG2E)).T.astype(BF16)
    k = rope(_dot(hb, w_ref[:, o:o + DA_WIDTH])); o += DA_WIDTH
    comp0 = (lax.broadcasted_iota(I32, k.shape, 1) % DA_V_DIM) < DA_HEAD_DIM
    k0_ref[...] = jnp.where(comp0, k, 0.0).astype(BF16)
    k1_ref[...] = jnp.where(comp0, 0.0, k).astype(BF16)
    vt_ref[0] = _dot(hb, w_ref[:, o:o + DA_WIDTH]).T.astype(BF16); o += DA_WIDTH
    u_ref[...] = _gelu(_dot(hb, w_ref[:, o:o + SG_WIDTH])).astype(BF16); o += SG_WIDTH
    sv = _gelu(_dot(hb, w_ref[:, o:o + SG_WIDTH])); o += SG_WIDTH
    mu = jnp.mean(sv, axis=-1, keepdims=True)
    svc = sv - mu
    sv_ref[...] = (svc * lax.rsqrt(jnp.mean(svc * svc, axis=-1, keepdims=True) + EPS) * lng_ref[...]
                   + lnb_ref[...]).astype(BF16)
    gate_ref[...] = jax.nn.sigmoid(_dot(hb, w_ref[:, o:o + 2 * D_MODEL]) + bg_ref[...]).astype(BF16)


def _proj_call(x2, g1, w_cat, b_gate, cos_t, sin_t, ln_g, ln_b, seq, tm):
    t = x2.shape[0]
    nseq = seq // tm
    row = lambda i: (i, 0)
    fixed = lambda i: (0, 0)
    pos = lambda i: (i % nseq, 0)
    wcols = w_cat.shape[1]
    transposed = jax.ShapeDtypeStruct((t // tm, DA_WIDTH, tm), BF16)
    tblk = pl.BlockSpec((1, DA_WIDTH, tm), lambda i: (i, 0, 0))
    out_shapes = [transposed] + [jax.ShapeDtypeStruct((t, DA_WIDTH), BF16)] * 2 + [transposed] + \
                 [jax.ShapeDtypeStruct((t, SG_WIDTH), BF16)] * 2 + \
                 [jax.ShapeDtypeStruct((t, 2 * D_MODEL), BF16)]
    return pl.pallas_call(
        _proj_kernel,
        grid=(t // tm,),
        in_specs=[
            pl.BlockSpec((tm, D_MODEL), row),
            pl.BlockSpec((1, D_MODEL), fixed),
            pl.BlockSpec((D_MODEL, wcols), fixed, pipeline_mode=pl.Buffered(1)),
            pl.BlockSpec((1, 2 * D_MODEL), fixed),
            pl.BlockSpec((tm, LANES), pos),
            pl.BlockSpec((tm, LANES), pos),
            pl.BlockSpec((1, SG_WIDTH), fixed),
            pl.BlockSpec((1, SG_WIDTH), fixed),
        ],
        out_specs=[tblk] + [pl.BlockSpec((tm, DA_WIDTH), row)] * 2 + [tblk] +
                  [pl.BlockSpec((tm, SG_WIDTH), row)] * 2 +
                  [pl.BlockSpec((tm, 2 * D_MODEL), row)],
        out_shape=out_shapes,
        compiler_params=pltpu.CompilerParams(dimension_semantics=("arbitrary",), vmem_limit_bytes=VMEM_LIMIT),
        name="proj",
    )(x2, g1, w_cat, b_gate, cos_t, sin_t, ln_g, ln_b)


def _attn_kernel(lam_ref, qt_ref, k0_ref, k1_ref, vt_ref, sg_ref, o_ref, acc_ref, st_ref, m_ref, mprev_ref, l_ref,
                 *, tq):
    i = pl.program_id(2)
    qt = qt_ref[0]

    m_ref[...] = jnp.full(m_ref.shape, -1e30, F32)
    l_ref[...] = jnp.zeros(l_ref.shape, F32)
    acc_ref[...] = jnp.zeros(acc_ref.shape, F32)

    def scores(j, masked):
        start = pl.multiple_of(j * tq, tq)
        if masked:
            keyc = lax.broadcasted_iota(I32, (tq, tq), 0) // CHUNK
            qryc = lax.broadcasted_iota(I32, (tq, tq), 1) // CHUNK
            visible = keyc <= qryc
        for c, k_ref in enumerate((k0_ref, k1_ref)):
            st = _dot(k_ref[pl.ds(start, tq), :], qt)
            if masked:
                st = jnp.where(visible, st, -jnp.inf)
            st_ref[c] = st
            m_old = m_ref[c]
            mprev_ref[c] = m_old
            m_ref[c] = jnp.maximum(m_old, jnp.max(st, axis=0, keepdims=True))

    def accumulate(j):
        vt = vt_ref[j]
        for c in range(2):
            m_new = m_ref[c]
            alpha = jnp.exp2(mprev_ref[c] - m_new)
            p = jnp.exp2(st_ref[c] - m_new)
            l_ref[c] = alpha * l_ref[c] + jnp.sum(p, axis=0, keepdims=True)
            acc_ref[c] = alpha * acc_ref[c] + _dot(vt, p.astype(BF16))

    scores(i, True)

    def body(j, prev):
        accumulate(prev)
        scores(j, False)
        return j

    accumulate(lax.fori_loop(0, i, body, i))

    lam = (jnp.exp(jnp.sum(lam_ref[0:1, :] * lam_ref[1:2, :], axis=-1, keepdims=True))
           - jnp.exp(jnp.sum(lam_ref[2:3, :] * lam_ref[3:4, :], axis=-1, keepdims=True)) + LAMBDA_INIT)
    o = acc_ref[0] / l_ref[0] - lam * (acc_ref[1] / l_ref[1])
    o = o * lax.rsqrt(jnp.mean(o * o, axis=0, keepdims=True) + EPS)
    o_ref[...] = (o.T * sg_ref[...] * (1.0 - LAMBDA_INIT)).astype(BF16)


def _attn_call(lam4, qt, k0, k1, vt, subln_g, batch, tq):
    t = k0.shape[0]
    s = t // batch
    nq = s // tq
    kern = functools.partial(_attn_kernel, tq=tq)
    kblk = pl.BlockSpec((s, DA_V_DIM), lambda bi, h, i: (bi, h))
    stat = pltpu.VMEM((2, 1, tq), F32)
    return pl.pallas_call(
        kern,
        grid=(batch, DA_HEADS, nq),
        in_specs=[
            pl.BlockSpec((4, DA_HEAD_DIM), lambda bi, h, i: (0, 0)),
            pl.BlockSpec((1, DA_V_DIM, tq), lambda bi, h, i: (bi * nq + i, h, 0)),
            kblk, kblk,
            pl.BlockSpec((nq, DA_V_DIM, tq), lambda bi, h, i: (bi, h, 0)),
            pl.BlockSpec((1, DA_V_DIM), lambda bi, h, i: (0, 0)),
        ],
        out_specs=pl.BlockSpec((tq, DA_V_DIM), lambda bi, h, i: (bi * nq + i, h)),
        out_shape=jax.ShapeDtypeStruct((t, DA_WIDTH), BF16),
        scratch_shapes=[
            pltpu.VMEM((2, DA_V_DIM, tq), F32),
            pltpu.VMEM((2, tq, tq), F32),
            stat, stat, stat,
        ],
        compiler_params=pltpu.CompilerParams(dimension_semantics=("arbitrary",) * 3, vmem_limit_bytes=VMEM_LIMIT),
        name="attn",
    )(lam4, qt, k0, k1, vt, subln_g)


def _mix_kernel(x_ref, u_ref, sv_ref, gate_ref, ya_ref, ws_ref, bs_ref, wa_ref, wb_ref, wo_ref, g2_ref,
                wq_ref, sk_ref, x1_ref, h2_ref, sc_ref):
    tm = x_ref.shape[0]
    pc = lax.broadcasted_iota(I32, (SG_BLOCK, SG_BLOCK), 0) // CHUNK
    pr = lax.broadcasted_iota(I32, (SG_BLOCK, SG_BLOCK), 1) // CHUNK
    causal = pr <= pc
    rows = []
    for n in range(tm // SG_BLOCK):
        cols = []
        for g in range(SG_GROUPS):
            w = jnp.where(causal, ws_ref[g], 0.0).astype(BF16)
            vb = sv_ref[n * SG_BLOCK:(n + 1) * SG_BLOCK, g * SG_GROUP_DIM:(g + 1) * SG_GROUP_DIM]
            cols.append(_dot(w, vb))
        rows.append(jnp.concatenate(cols, axis=1) + bs_ref[...])
    s = jnp.concatenate(rows, axis=0)
    yb = (u_ref[...].astype(F32) * s).astype(BF16)
    merged = (gate_ref[:, :D_MODEL].astype(F32) * _dot(ya_ref[...], wa_ref[...])
              + gate_ref[:, D_MODEL:].astype(F32) * _dot(yb, wb_ref[...]))
    x1 = x_ref[...] + _dot(merged.astype(BF16), wo_ref[...])
    x1_ref[...] = x1
    h2 = _rms(x1, g2_ref[...]).astype(BF16)
    h2_ref[...] = h2
    qp = _dot(h2, wq_ref[...]).astype(BF16)
    for hp in range(2 * PEER_HEADS):
        sc_ref[hp] = _dot_t(sk_ref[hp], qp[:, hp * PEER_HALF:(hp + 1) * PEER_HALF])


def _mix_call(x2, u_act, sv_ln, gates, ya, w_s, b_s_full, w_a, w_b, w_o, g2, w_q, sk, tm):
    t = x2.shape[0]
    row = lambda i: (i, 0)
    fixed2 = lambda i: (0, 0)
    fixed3 = lambda i: (0, 0, 0)
    return pl.pallas_call(
        _mix_kernel,
        grid=(t // tm,),
        in_specs=[
            pl.BlockSpec((tm, D_MODEL), row),
            pl.BlockSpec((tm, SG_WIDTH), row),
            pl.BlockSpec((tm, SG_WIDTH), row),
            pl.BlockSpec((tm, 2 * D_MODEL), row),
            pl.BlockSpec((tm, DA_WIDTH), row),
            pl.BlockSpec((SG_GROUPS, SG_BLOCK, SG_BLOCK), fixed3),
            pl.BlockSpec((SG_BLOCK, SG_WIDTH), fixed2),
            pl.BlockSpec((DA_WIDTH, D_MODEL), fixed2),
            pl.BlockSpec((SG_WIDTH, D_MODEL), fixed2),
            pl.BlockSpec((D_MODEL, D_MODEL), fixed2),
            pl.BlockSpec((1, D_MODEL), fixed2),
            pl.BlockSpec((D_MODEL, 2 * PEER_HEADS * PEER_HALF), fixed2),
            pl.BlockSpec((2 * PEER_HEADS, PEER_NKEYS, PEER_HALF), fixed3),
        ],
        out_specs=[
            pl.BlockSpec((tm, D_MODEL), row),
            pl.BlockSpec((tm, D_MODEL), row),
            pl.BlockSpec((2 * PEER_HEADS, PEER_NKEYS, tm), lambda i: (0, 0, i)),
        ],
        out_shape=[
            jax.ShapeDtypeStruct((t, D_MODEL), F32),
            jax.ShapeDtypeStruct((t, D_MODEL), BF16),
            jax.ShapeDtypeStruct((2 * PEER_HEADS, PEER_NKEYS, t), F32),
        ],
        compiler_params=pltpu.CompilerParams(dimension_semantics=("arbitrary",), vmem_limit_bytes=VMEM_LIMIT),
        name="mix",
    )(x2, u_act, sv_ln, gates, ya, w_s, b_s_full, w_a, w_b, w_o, g2, w_q, sk)


def _top16(vals, n):
    tt = vals.shape[1]
    pos = lax.broadcasted_iota(I32, (n, tt), 0)
    nblk = n // SUBLANES
    row = lax.broadcasted_iota(I32, (SUBLANES, tt), 0)
    pos_blocks = [row + b * SUBLANES for b in range(nblk)]
    cur = vals
    out_v, out_i = [], []
    for _ in range(PEER_TOPK):
        level = [(cur[b * SUBLANES:(b + 1) * SUBLANES], pos_blocks[b]) for b in range(nblk)]
        while len(level) > 1:
            nxt = []
            for a in range(0, len(level) - 1, 2):
                (va, ia), (vb, ib) = level[a], level[a + 1]
                take_b = vb > va
                nxt.append((jnp.where(take_b, vb, va), jnp.where(take_b, ib, ia)))
            if len(level) % 2:
                nxt.append(level[-1])
            level = nxt
        v8, i8 = level[0]
        m = jnp.max(v8, axis=0, keepdims=True)
        idx = jnp.min(jnp.where(v8 == m, i8, n), axis=0, keepdims=True)
        out_v.append(m)
        out_i.append(idx)
        cur = jnp.where(pos == idx, -jnp.inf, cur)
    return out_v, out_i


def _topk_kernel(sc_ref, row_ref, g_ref):
    tt = sc_ref.shape[2]
    k = PEER_TOPK
    rows_all, g_all = [], []
    for h in range(PEER_HEADS):
        s1, i1 = _top16(sc_ref[2 * h], PEER_NKEYS)
        s2, i2 = _top16(sc_ref[2 * h + 1], PEER_NKEYS)
        s2m = jnp.concatenate(s2, axis=0)
        i2m = jnp.concatenate(i2, axis=0)
        keep = [k // (a + 1) for a in range(k)]
        n_cand = -(-sum(keep) // SUBLANES) * SUBLANES
        pad = n_cand - sum(keep)
        cand = jnp.concatenate([s1[a] + s2m[:keep[a]] for a in range(k)]
                               + [jnp.full((pad, tt), -jnp.inf, F32)], axis=0)
        cidx = jnp.concatenate([i1[a] * PEER_NKEYS + i2m[:keep[a]] for a in range(k)]
                               + [jnp.zeros((pad, tt), I32)], axis=0)
        top_s, pos = _top16(cand, n_cand)
        cpos = lax.broadcasted_iota(I32, (n_cand, tt), 0)
        e = [jnp.sum(jnp.where(cpos == pos[r], cidx, 0), axis=0, keepdims=True) for r in range(k)]
        w = [jnp.exp(top_s[r] - top_s[0]) for r in range(k)]
        den = w[0]
        for r in range(1, k):
            den = den + w[r]
        rows_all.append(jnp.concatenate(e, axis=0) * ROW_WORDS)
        g_all.append(jnp.concatenate(w, axis=0) / den)
    row_ref[...] = jnp.concatenate(rows_all, axis=0).T
    g_ref[...] = jnp.concatenate(g_all, axis=0).T


def _topk_call(sc, tt):
    t = sc.shape[2]
    blk = pl.BlockSpec((tt, PEER_SLOTS), lambda i: (i, 0))
    return pl.pallas_call(
        _topk_kernel,
        grid=(t // tt,),
        in_specs=[pl.BlockSpec((2 * PEER_HEADS, PEER_NKEYS, tt), lambda i: (0, 0, i))],
        out_specs=[blk, blk],
        out_shape=[jax.ShapeDtypeStruct((t, PEER_SLOTS), I32),
                   jax.ShapeDtypeStruct((t, PEER_SLOTS), F32)],
        compiler_params=pltpu.CompilerParams(dimension_semantics=("arbitrary",), vmem_limit_bytes=VMEM_LIMIT),
        name="topk",
    )(sc)


def _chunk_operand(tile_ref, c):
    return pltpu.bitcast(tile_ref[pl.ds(c, PEER_SLOTS, stride=ROW_WORDS), :], BF16)


def _gather_group(idx_ref, slot, tab_ref, tiles):
    for s in range(PEER_GROUP):
        for j in range(PEER_SLOTS):
            r = pl.multiple_of(idx_ref[slot, s, j], ROW_WORDS)
            tiles[s][pl.ds(j * ROW_WORDS, ROW_WORDS), :] = tab_ref[pl.ds(r, ROW_WORDS), :]


def _pipelined_groups(rows_hbm, idx_ref, sem, tab_ref, tiles, n_groups, compute):
    first = pl.program_id(0) * n_groups

    def idx_copy(g, slot):
        src = rows_hbm.at[first + jnp.minimum(g, n_groups - 1)]
        return pltpu.make_async_copy(src, idx_ref.at[slot], sem.at[slot])

    for g in range(IDX_SLOTS - 1):
        idx_copy(g, g).start()
    idx_copy(0, 0).wait()
    _gather_group(idx_ref, 0, tab_ref, tiles)
    idx_copy(IDX_SLOTS - 1, IDX_SLOTS - 1).start()

    def body(it, carry):
        for k in range(IDX_SLOTS):
            g = IDX_SLOTS * it + k
            nxt = (k + 1) % IDX_SLOTS
            idx_copy(g + 1, nxt).wait()
            compute(g)
            _gather_group(idx_ref, nxt, tab_ref, tiles)
            idx_copy(g + IDX_SLOTS, k).start()
        return carry

    lax.fori_loop(0, n_groups // IDX_SLOTS, body, 0)
    for slot in range(1, IDX_SLOTS):
        idx_copy(n_groups, slot).wait()


def _group_rows(g):
    t0 = g * PEER_GROUP
    base = pl.multiple_of((t0 // PACK_ROWS) * PACK_ROWS, PACK_ROWS)
    return base, t0 - base


def _peer_u_kernel(rows_hbm, h2_ref, g_ref, tab_ref, coef_ref, idx_ref, sem, *scratch):
    tiles, acc_lo, acc_hi = scratch[:PEER_GROUP], scratch[PEER_GROUP], scratch[PEER_GROUP + 1]
    tt = h2_ref.shape[0]
    width = 2 * PEER_SLOTS
    sub = lax.broadcasted_iota(I32, (PACK_ROWS, width), 0)
    acc_lo[...] = jnp.zeros(acc_lo.shape, F32)
    acc_hi[...] = jnp.zeros(acc_hi.shape, F32)

    def compute(g):
        base, r0 = _group_rows(g)
        xg = h2_ref[pl.ds(base, PACK_ROWS), :]
        lhs = [jnp.concatenate([xg[:, c * LANES:(c + 1) * LANES],
                                xg[:, D_HALF + c * LANES:D_HALF + (c + 1) * LANES]], axis=0)
               for c in range(ROW_WORDS)]
        lo = acc_lo[pl.ds(base, PACK_ROWS), :]
        hi = acc_hi[pl.ds(base, PACK_ROWS), :]
        for s in range(PEER_GROUP):
            res = _dot_t(lhs[0], _chunk_operand(tiles[s], 0))
            for c in range(1, ROW_WORDS):
                res = res + _dot_t(lhs[c], _chunk_operand(tiles[s], c))
            lo = jnp.where(sub == r0 + s, res[:PACK_ROWS], lo)
            hi = jnp.where(sub == r0 + s, res[PACK_ROWS:], hi)
        acc_lo[pl.ds(base, PACK_ROWS), :] = lo
        acc_hi[pl.ds(base, PACK_ROWS), :] = hi

    _pipelined_groups(rows_hbm, idx_ref, sem, tab_ref, tiles, tt // PEER_GROUP, compute)
    col = lax.broadcasted_iota(I32, (tt, width), 1)
    halves = jnp.where(col % 2 == 0, acc_lo[...], acc_hi[...])
    act = _gelu(halves + pltpu.roll(halves, width - 1, 1))
    src = lax.broadcasted_iota(I32, (PEER_SLOTS, width), 0)
    dst = lax.broadcasted_iota(I32, (PEER_SLOTS, width), 1)
    spread = jnp.where(dst == 2 * src, 1.0, 0.0).astype(BF16)
    coef = act * _dot(g_ref[...].astype(BF16), spread)
    coef_ref[:, :width] = coef.astype(BF16)
    coef_ref[:, width:] = pltpu.roll(coef, 1, 1).astype(BF16)


def _gather_scratch():
    return ([pltpu.SMEM((IDX_SLOTS, PEER_GROUP, PEER_SLOTS), I32), pltpu.SemaphoreType.DMA((IDX_SLOTS,))]
            + [pltpu.VMEM((ROW_WORDS * PEER_SLOTS, LANES), I32) for _ in range(PEER_GROUP)])


def _peer_u_call(rows, h2, g, u_tab, tt):
    t = h2.shape[0]
    tok = lambda i: (i, 0)
    width = 2 * PEER_SLOTS
    return pl.pallas_call(
        _peer_u_kernel,
        grid=(t // tt,),
        in_specs=[
            pl.BlockSpec(memory_space=pl.ANY),
            pl.BlockSpec((tt, D_MODEL), tok),
            pl.BlockSpec((tt, PEER_SLOTS), tok),
            pl.BlockSpec(u_tab.shape, lambda i: (0, 0), pipeline_mode=pl.Buffered(1)),
        ],
        out_specs=pl.BlockSpec((tt, 2 * width), tok),
        out_shape=jax.ShapeDtypeStruct((t, 2 * width), BF16),
        scratch_shapes=_gather_scratch() + [pltpu.VMEM((tt, width), F32)] * 2,
        compiler_params=pltpu.CompilerParams(dimension_semantics=("arbitrary",),
                                             vmem_limit_bytes=VMEM_LIMIT_TABLE),
        name="peer_u",
    )(rows, h2, g, u_tab)


def _peer_v_kernel(rows_hbm, coef_ref, x1_ref, gf_ref, tab_ref, out_ref, idx_ref, sem, *scratch):
    tiles, acc_ref = scratch[:PEER_GROUP], scratch[PEER_GROUP]
    tt = coef_ref.shape[0]
    width = 2 * PEER_SLOTS
    sub = lax.broadcasted_iota(I32, (PACK_ROWS, LANES), 0)
    acc_ref[...] = jnp.zeros(acc_ref.shape, F32)

    def compute(g):
        base, r0 = _group_rows(g)
        cg = jnp.concatenate([coef_ref[pl.ds(base, PACK_ROWS), :width],
                              coef_ref[pl.ds(base, PACK_ROWS), width:]], axis=0)
        cols = [h * D_HALF + c * LANES for c in range(ROW_WORDS) for h in range(2)]
        outs = [acc_ref[pl.ds(base, PACK_ROWS), o:o + LANES] for o in cols]
        for s in range(PEER_GROUP):
            for c in range(ROW_WORDS):
                res = _dot(cg, _chunk_operand(tiles[s], c))
                outs[2 * c] = jnp.where(sub == r0 + s, res[:PACK_ROWS], outs[2 * c])
                outs[2 * c + 1] = jnp.where(sub == r0 + s, res[PACK_ROWS:], outs[2 * c + 1])
        for o, val in zip(cols, outs):
            acc_ref[pl.ds(base, PACK_ROWS), o:o + LANES] = val

    _pipelined_groups(rows_hbm, idx_ref, sem, tab_ref, tiles, tt // PEER_GROUP, compute)
    out_ref[...] = _rms(x1_ref[...] + acc_ref[...], gf_ref[...])


def _peer_v_call(rows, coef, x1, gf, v_tab, tt):
    t = x1.shape[0]
    tok = lambda i: (i, 0)
    return pl.pallas_call(
        _peer_v_kernel,
        grid=(t // tt,),
        in_specs=[
            pl.BlockSpec(memory_space=pl.ANY),
            pl.BlockSpec((tt, 4 * PEER_SLOTS), tok),
            pl.BlockSpec((tt, D_MODEL), tok),
            pl.BlockSpec((1, D_MODEL), lambda i: (0, 0)),
            pl.BlockSpec(v_tab.shape, lambda i: (0, 0), pipeline_mode=pl.Buffered(1)),
        ],
        out_specs=pl.BlockSpec((tt, D_MODEL), tok),
        out_shape=jax.ShapeDtypeStruct((t, D_MODEL), F32),
        scratch_shapes=_gather_scratch() + [pltpu.VMEM((tt, D_MODEL), F32)],
        compiler_params=pltpu.CompilerParams(dimension_semantics=("arbitrary",),
                                             vmem_limit_bytes=VMEM_LIMIT_TABLE),
        name="peer_v",
    )(rows, coef, x1, gf, v_tab)


def _pack_rows(tab):
    bits = lax.bitcast_convert_type(tab.astype(BF16), jnp.uint16).astype(jnp.uint32)
    word = bits[:, :D_HALF] | (bits[:, D_HALF:] << 16)
    return lax.bitcast_convert_type(word, I32).reshape(tab.shape[0] * ROW_WORDS, LANES)


def _rope_tables(seq):
    half = DA_HEAD_DIM // 2
    inv = 1.0 / (ROPE_THETA ** (jnp.arange(0, DA_HEAD_DIM, 2, dtype=F32) / DA_HEAD_DIM))
    ang = jnp.arange(seq, dtype=F32)[:, None] * inv[None, :]
    cos, sin = jnp.cos(ang), jnp.sin(ang)
    reps = LANES // DA_HEAD_DIM
    cos_t = jnp.tile(jnp.concatenate([cos, cos], axis=1), (1, reps))
    sin_t = jnp.tile(jnp.concatenate([-sin, sin], axis=1), (1, reps))
    del half
    return cos_t, sin_t


def kernel(x, norm1_g, w_in, lambda_q1, lambda_k1, lambda_q2, lambda_k2, subln_g, sgu_ln_g, sgu_ln_b, w_spatial, b_spatial, w_branch_a, w_branch_b, w_gate, b_gate, w_out, norm2_g, peer_wq, peer_subkeys, peer_u, peer_v, norm_f_g):
    b, s, d = x.shape
    t = b * s
    l = 0
    x2 = x.reshape(t, d)
    cos_t, sin_t = _rope_tables(s)
    w_cat = jnp.concatenate([w_in[l], w_gate[l]], axis=1).astype(BF16)

    tq = min(ATTN_TQ, s)
    q, k0, k1, vt, u_act, sv_ln, gates = _proj_call(
        x2, norm1_g[l][None], w_cat, b_gate[l][None], cos_t, sin_t, sgu_ln_g[l][None], sgu_ln_b[l][None],
        seq=s, tm=tq)

    lam4 = jnp.stack([lambda_q1[l], lambda_k1[l], lambda_q2[l], lambda_k2[l]]).astype(F32)
    ya = _attn_call(lam4, q, k0, k1, vt, subln_g[l][None], batch=b, tq=tq)

    b_s_full = jnp.repeat(b_spatial[l].T, SG_GROUP_DIM, axis=1)
    sk = peer_subkeys[l].reshape(2 * PEER_HEADS, PEER_NKEYS, PEER_HALF).astype(BF16)
    x1, h2, sc = _mix_call(x2, u_act, sv_ln, gates, ya, w_spatial[l], b_s_full,
                           w_branch_a[l].astype(BF16), w_branch_b[l].astype(BF16), w_out[l].astype(BF16),
                           norm2_g[l][None], peer_wq[l].astype(BF16), sk, tm=min(MIX_TM, t))

    rows, g = _topk_call(sc, tt=min(TOPK_TT, t))
    rows = rows.reshape(t // PEER_GROUP, PEER_GROUP, PEER_SLOTS)
    coef = _peer_u_call(rows, h2, g, _pack_rows(peer_u[l]), tt=min(PEER_TT, t))
    out = _peer_v_call(rows, coef, x1, norm_f_g[None], _pack_rows(peer_v[l]), tt=min(PEER_TT, t))
    return out.reshape(b, s, d)
```

```python
import functools
import math

import jax
import jax.numpy as jnp
from jax import lax
from jax.experimental import pallas as pl
from jax.experimental.pallas import tpu as pltpu

F32 = jnp.float32
BF16 = jnp.bfloat16
I32 = jnp.int32

D_MODEL = 1024
CHUNK = 64
ROPE_THETA = 10000.0
EPS = 1e-6

DA_HEADS = 4
DA_HEAD_DIM = 64
DA_V_DIM = 2 * DA_HEAD_DIM
DA_WIDTH = DA_HEADS * DA_V_DIM

SG_GROUPS = 4
SG_BLOCK = 128
SG_GROUP_DIM = 128
SG_WIDTH = SG_GROUPS * SG_GROUP_DIM

PEER_HEADS = 8
PEER_NKEYS = 128
PEER_HALF = 128
PEER_TOPK = 16
PEER_SLOTS = PEER_HEADS * PEER_TOPK
PEER_N_EXPERTS = PEER_NKEYS * PEER_NKEYS

LANES = 128
SUBLANES = 8
PACK_ROWS = 16
PEER_GROUP = 8
D_HALF = D_MODEL // 2
ROW_WORDS = D_HALF // LANES
IDX_SLOTS = 4

ATTN_TQ = 1024
MIX_TM = 512
TOPK_TT = 256
PEER_TT = 512

VMEM_LIMIT = 48 * 1024 * 1024
VMEM_LIMIT_TABLE = 56 * 1024 * 1024

LAMBDA_INIT = 0.8 - 0.6 * math.exp(-0.3 * 0)
LOG2E = math.log2(math.e)
NEG_BIG = -1e30


def _gelu(x):
    return 0.5 * x * (1.0 + lax.erf(x * (0.5 ** 0.5)))


def _rms(x, g):
    return x * lax.rsqrt(jnp.mean(x * x, axis=-1, keepdims=True) + EPS) * g


def _dot(a, b):
    return jnp.dot(a, b, preferred_element_type=F32)


def _dot_t(a, b):
    return lax.dot_general(a, b, (((1,), (1,)), ((), ())), preferred_element_type=F32)


def _proj_kernel(x_ref, g1_ref, w_ref, bg_ref, cos_ref, sin_ref, lng_ref, lnb_ref,
                 qt_ref, k0_ref, k1_ref, vt_ref, u_ref, sv_ref, gate_ref):
    tm = x_ref.shape[0]
    hb = _rms(x_ref[...], g1_ref[...]).astype(BF16)
    cos = cos_ref[...]
    sin = sin_ref[...]
    lane = lax.broadcasted_iota(I32, (tm, LANES), 1)
    first_half = (lane % DA_HEAD_DIM) < (DA_HEAD_DIM // 2)

    def rope(t):
        outs = []
        for b in range(DA_WIDTH // LANES):
            blk = t[:, b * LANES:(b + 1) * LANES]
            swapped = jnp.where(first_half, pltpu.roll(blk, LANES - DA_HEAD_DIM // 2, 1),
                                pltpu.roll(blk, DA_HEAD_DIM // 2, 1))
            outs.append(blk * cos + swapped * sin)
        return jnp.concatenate(outs, axis=1)

    o = 0
    q = _dot(hb, w_ref[:, o:o + DA_WIDTH]); o += DA_WIDTH
    qt_ref[0] = (rope(q) * (DA_HEAD_DIM ** -0.5 * LOG2E)).T.astype(BF16)
    k = rope(_dot(hb, w_ref[:, o:o + DA_WIDTH])); o += DA_WIDTH
    comp0 = (lax.broadcasted_iota(I32, k.shape, 1) % DA_V_DIM) < DA_HEAD_DIM
    k0_ref[...] = jnp.where(comp0, k, 0.0).astype(BF16)
    k1_ref[...] = jnp.where(comp0, 0.0, k).astype(BF16)
    vt_ref[0] = _dot(hb, w_ref[:, o:o + DA_WIDTH]).T.astype(BF16); o += DA_WIDTH
    u_ref[...] = _gelu(_dot(hb, w_ref[:, o:o + SG_WIDTH])).astype(BF16); o += SG_WIDTH
    sv = _gelu(_dot(hb, w_ref[:, o:o + SG_WIDTH])); o += SG_WIDTH
    mu = jnp.mean(sv, axis=-1, keepdims=True)
    svc = sv - mu
    sv_ref[...] = (svc * lax.rsqrt(jnp.mean(svc * svc, axis=-1, keepdims=True) + EPS) * lng_ref[...]
                   + lnb_ref[...]).astype(BF16)
    gate_ref[...] = jax.nn.sigmoid(_dot(hb, w_ref[:, o:o + 2 * D_MODEL]) + bg_ref[...]).astype(BF16)


def _proj_call(x2, g1, w_cat, b_gate, cos_t, sin_t, ln_g, ln_b, seq, tm):
    t = x2.shape[0]
    nseq = seq // tm
    row = lambda i: (i, 0)
    fixed = lambda i: (0, 0)
    pos = lambda i: (i % nseq, 0)
    wcols = w_cat.shape[1]
    transposed = jax.ShapeDtypeStruct((t // tm, DA_WIDTH, tm), BF16)
    tblk = pl.BlockSpec((1, DA_WIDTH, tm), lambda i: (i, 0, 0))
    out_shapes = [transposed] + [jax.ShapeDtypeStruct((t, DA_WIDTH), BF16)] * 2 + [transposed] + \
                 [jax.ShapeDtypeStruct((t, SG_WIDTH), BF16)] * 2 + \
                 [jax.ShapeDtypeStruct((t, 2 * D_MODEL), BF16)]
    return pl.pallas_call(
        _proj_kernel,
        grid=(t // tm,),
        in_specs=[
            pl.BlockSpec((tm, D_MODEL), row),
            pl.BlockSpec((1, D_MODEL), fixed),
            pl.BlockSpec((D_MODEL, wcols), fixed, pipeline_mode=pl.Buffered(1)),
            pl.BlockSpec((1, 2 * D_MODEL), fixed),
            pl.BlockSpec((tm, LANES), pos),
            pl.BlockSpec((tm, LANES), pos),
            pl.BlockSpec((1, SG_WIDTH), fixed),
            pl.BlockSpec((1, SG_WIDTH), fixed),
        ],
        out_specs=[tblk] + [pl.BlockSpec((tm, DA_WIDTH), row)] * 2 + [tblk] +
                  [pl.BlockSpec((tm, SG_WIDTH), row)] * 2 +
                  [pl.BlockSpec((tm, 2 * D_MODEL), row)],
        out_shape=out_shapes,
        compiler_params=pltpu.CompilerParams(dimension_semantics=("arbitrary",), vmem_limit_bytes=VMEM_LIMIT),
        name="proj",
    )(x2, g1, w_cat, b_gate, cos_t, sin_t, ln_g, ln_b)


def _attn_kernel(lam_ref, qt_ref, k0_ref, k1_ref, vt_ref, sg_ref, o_ref, acc_ref, st_ref, m_ref, mprev_ref, l_ref,
                 *, tq):
    i = pl.program_id(2)
    qt = qt_ref[0]

    m_ref[...] = jnp.full(m_ref.shape, NEG_BIG, F32)
    l_ref[...] = jnp.zeros(l_ref.shape, F32)
    acc_ref[...] = jnp.zeros(acc_ref.shape, F32)

    def scores(j, masked):
        start = pl.multiple_of(j * tq, tq)
        if masked:
            keyc = lax.broadcasted_iota(I32, (tq, tq), 0) // CHUNK
            qryc = lax.broadcasted_iota(I32, (tq, tq), 1) // CHUNK
            visible = keyc <= qryc
        for c, k_ref in enumerate((k0_ref, k1_ref)):
            st = _dot(k_ref[pl.ds(start, tq), :], qt)
            if masked:
                st = jnp.where(visible, st, -jnp.inf)
            st_ref[c] = st
            m_old = m_ref[c]
            mprev_ref[c] = m_old
            m_ref[c] = jnp.maximum(m_old, jnp.max(st, axis=0, keepdims=True))

    def accumulate(j):
        vt = vt_ref[j]
        for c in range(2):
            m_new = m_ref[c]
            alpha = jnp.exp2(mprev_ref[c] - m_new)
            p = jnp.exp2(st_ref[c] - m_new)
            l_ref[c] = alpha * l_ref[c] + jnp.sum(p, axis=0, keepdims=True)
            acc_ref[c] = alpha * acc_ref[c] + _dot(vt, p.astype(BF16))

    scores(i, True)

    def body(j, prev):
        accumulate(prev)
        scores(j, False)
        return j

    accumulate(lax.fori_loop(0, i, body, i))

    lam = (jnp.exp(jnp.sum(lam_ref[0:1, :] * lam_ref[1:2, :], axis=-1, keepdims=True))
           - jnp.exp(jnp.sum(lam_ref[2:3, :] * lam_ref[3:4, :], axis=-1, keepdims=True)) + LAMBDA_INIT)
    o = acc_ref[0] / l_ref[0] - lam * (acc_ref[1] / l_ref[1])
    o = o * lax.rsqrt(jnp.mean(o * o, axis=0, keepdims=True) + EPS)
    o_ref[...] = (o.T * sg_ref[...] * (1.0 - LAMBDA_INIT)).astype(BF16)


def _attn_call(lam4, qt, k0, k1, vt, subln_g, batch, tq):
    t = k0.shape[0]
    s = t // batch
    nq = s // tq
    kern = functools.partial(_attn_kernel, tq=tq)
    kblk = pl.BlockSpec((s, DA_V_DIM), lambda bi, h, i: (bi, h))
    stat = pltpu.VMEM((2, 1, tq), F32)
    return pl.pallas_call(
        kern,
        grid=(batch, DA_HEADS, nq),
        in_specs=[
            pl.BlockSpec((4, DA_HEAD_DIM), lambda bi, h, i: (0, 0)),
            pl.BlockSpec((1, DA_V_DIM, tq), lambda bi, h, i: (bi * nq + i, h, 0)),
            kblk, kblk,
            pl.BlockSpec((nq, DA_V_DIM, tq), lambda bi, h, i: (bi, h, 0)),
            pl.BlockSpec((1, DA_V_DIM), lambda bi, h, i: (0, 0)),
        ],
        out_specs=pl.BlockSpec((tq, DA_V_DIM), lambda bi, h, i: (bi * nq + i, h)),
        out_shape=jax.ShapeDtypeStruct((t, DA_WIDTH), BF16),
        scratch_shapes=[
            pltpu.VMEM((2, DA_V_DIM, tq), F32),
            pltpu.VMEM((2, tq, tq), F32),
            stat, stat, stat,
        ],
        compiler_params=pltpu.CompilerParams(dimension_semantics=("arbitrary",) * 3, vmem_limit_bytes=VMEM_LIMIT),
        name="attn",
    )(lam4, qt, k0, k1, vt, subln_g)


def _mix_kernel(x_ref, u_ref, sv_ref, gate_ref, ya_ref, ws_ref, bs_ref, wa_ref, wb_ref, wo_ref, g2_ref,
                wq_ref, sk_ref, x1_ref, h2_ref, sc_ref):
    tm = x_ref.shape[0]
    pc = lax.broadcasted_iota(I32, (SG_BLOCK, SG_BLOCK), 0) // CHUNK
    pr = lax.broadcasted_iota(I32, (SG_BLOCK, SG_BLOCK), 1) // CHUNK
    causal = pr <= pc
    rows = []
    for n in range(tm // SG_BLOCK):
        cols = []
        for g in range(SG_GROUPS):
            w = jnp.where(causal, ws_ref[g], 0.0).astype(BF16)
            vb = sv_ref[n * SG_BLOCK:(n + 1) * SG_BLOCK, g * SG_GROUP_DIM:(g + 1) * SG_GROUP_DIM]
            cols.append(_dot(w, vb))
        rows.append(jnp.concatenate(cols, axis=1) + bs_ref[...])
    s = jnp.concatenate(rows, axis=0)
    yb = (u_ref[...].astype(F32) * s).astype(BF16)
    merged = (gate_ref[:, :D_MODEL].astype(F32) * _dot(ya_ref[...], wa_ref[...])
              + gate_ref[:, D_MODEL:].astype(F32) * _dot(yb, wb_ref[...]))
    x1 = x_ref[...] + _dot(merged.astype(BF16), wo_ref[...])
    x1_ref[...] = x1
    h2 = _rms(x1, g2_ref[...]).astype(BF16)
    h2_ref[...] = h2
    qp = _dot(h2, wq_ref[...]).astype(BF16)
    for hp in range(2 * PEER_HEADS):
        sc_ref[hp] = _dot_t(sk_ref[hp], qp[:, hp * PEER_HALF:(hp + 1) * PEER_HALF])


def _mix_call(x2, u_act, sv_ln, gates, ya, w_s, b_s_full, w_a, w_b, w_o, g2, w_q, sk, tm):
    t = x2.shape[0]
    row = lambda i: (i, 0)
    fixed2 = lambda i: (0, 0)
    fixed3 = lambda i: (0, 0, 0)
    return pl.pallas_call(
        _mix_kernel,
        grid=(t // tm,),
        in_specs=[
            pl.BlockSpec((tm, D_MODEL), row),
            pl.BlockSpec((tm, SG_WIDTH), row),
            pl.BlockSpec((tm, SG_WIDTH), row),
            pl.BlockSpec((tm, 2 * D_MODEL), row),
            pl.BlockSpec((tm, DA_WIDTH), row),
            pl.BlockSpec((SG_GROUPS, SG_BLOCK, SG_BLOCK), fixed3),
            pl.BlockSpec((SG_BLOCK, SG_WIDTH), fixed2),
            pl.BlockSpec((DA_WIDTH, D_MODEL), fixed2),
            pl.BlockSpec((SG_WIDTH, D_MODEL), fixed2),
            pl.BlockSpec((D_MODEL, D_MODEL), fixed2),
            pl.BlockSpec((1, D_MODEL), fixed2),
            pl.BlockSpec((D_MODEL, 2 * PEER_HEADS * PEER_HALF), fixed2),
            pl.BlockSpec((2 * PEER_HEADS, PEER_NKEYS, PEER_HALF), fixed3),
        ],
        out_specs=[
            pl.BlockSpec((tm, D_MODEL), row),
            pl.BlockSpec((tm, D_MODEL), row),
            pl.BlockSpec((2 * PEER_HEADS, PEER_NKEYS, tm), lambda i: (0, 0, i)),
        ],
        out_shape=[
            jax.ShapeDtypeStruct((t, D_MODEL), F32),
            jax.ShapeDtypeStruct((t, D_MODEL), BF16),
            jax.ShapeDtypeStruct((2 * PEER_HEADS, PEER_NKEYS, t), F32),
        ],
        compiler_params=pltpu.CompilerParams(dimension_semantics=("arbitrary",), vmem_limit_bytes=VMEM_LIMIT),
        name="mix",
    )(x2, u_act, sv_ln, gates, ya, w_s, b_s_full, w_a, w_b, w_o, g2, w_q, sk)


def _top16(vals, n):
    tt = vals.shape[1]
    pos = lax.broadcasted_iota(I32, (n, tt), 0)
    nblk = n // SUBLANES
    row = lax.broadcasted_iota(I32, (SUBLANES, tt), 0)
    pos_blocks = [row + b * SUBLANES for b in range(nblk)]
    cur = vals
    out_v, out_i = [], []
    for _ in range(PEER_TOPK):
        level = [(cur[b * SUBLANES:(b + 1) * SUBLANES], pos_blocks[b]) for b in range(nblk)]
        while len(level) > 1:
            nxt = []
            for a in range(0, len(level) - 1, 2):
                (va, ia), (vb, ib) = level[a], level[a + 1]
                take_b = vb > va
                nxt.append((jnp.where(take_b, vb, va), jnp.where(take_b, ib, ia)))
            if len(level) % 2:
                nxt.append(level[-1])
            level = nxt
        v8, i8 = level[0]
        m = jnp.max(v8, axis=0, keepdims=True)
        idx = jnp.min(jnp.where(v8 == m, i8, n), axis=0, keepdims=True)
        out_v.append(m)
        out_i.append(idx)
        cur = jnp.where(pos == idx, -jnp.inf, cur)
    return out_v, out_i


def _topk_kernel(sc_ref, row_ref, g_ref):
    tt = sc_ref.shape[2]
    k = PEER_TOPK
    rows_all, g_all = [], []
    for h in range(PEER_HEADS):
        s1, i1 = _top16(sc_ref[2 * h], PEER_NKEYS)
        s2, i2 = _top16(sc_ref[2 * h + 1], PEER_NKEYS)
        s2m = jnp.concatenate(s2, axis=0)
        i2m = jnp.concatenate(i2, axis=0)
        keep = [k // (a + 1) for a in range(k)]
        n_cand = -(-sum(keep) // SUBLANES) * SUBLANES
        pad = n_cand - sum(keep)
        cand = jnp.concatenate([s1[a] + s2m[:keep[a]] for a in range(k)]
                               + [jnp.full((pad, tt), -jnp.inf, F32)], axis=0)
        cidx = jnp.concatenate([i1[a] * PEER_NKEYS + i2m[:keep[a]] for a in range(k)]
                               + [jnp.zeros((pad, tt), I32)], axis=0)
        top_s, pos = _top16(cand, n_cand)
        cpos = lax.broadcasted_iota(I32, (n_cand, tt), 0)
        e = [jnp.sum(jnp.where(cpos == pos[r], cidx, 0), axis=0, keepdims=True) for r in range(k)]
        w = [jnp.exp(top_s[r] - top_s[0]) for r in range(k)]
        den = w[0]
        for r in range(1, k):
            den = den + w[r]
        rows_all.append(jnp.concatenate(e, axis=0) * ROW_WORDS)
        g_all.append(jnp.concatenate(w, axis=0) / den)
    row_ref[...] = jnp.concatenate(rows_all, axis=0).T
    g_ref[...] = jnp.concatenate(g_all, axis=0).T


def _topk_call(sc, tt):
    t = sc.shape[2]
    blk = pl.BlockSpec((tt, PEER_SLOTS), lambda i: (i, 0))
    return pl.pallas_call(
        _topk_kernel,
        grid=(t // tt,),
        in_specs=[pl.BlockSpec((2 * PEER_HEADS, PEER_NKEYS, tt), lambda i: (0, 0, i))],
        out_specs=[blk, blk],
        out_shape=[jax.ShapeDtypeStruct((t, PEER_SLOTS), I32),
                   jax.ShapeDtypeStruct((t, PEER_SLOTS), F32)],
        compiler_params=pltpu.CompilerParams(dimension_semantics=("arbitrary",), vmem_limit_bytes=VMEM_LIMIT),
        name="topk",
    )(sc)


def _chunk_operand(tile_ref, c):
    return pltpu.bitcast(tile_ref[pl.ds(c, PEER_SLOTS, stride=ROW_WORDS), :], BF16)


def _gather_group(idx_ref, slot, tab_ref, tiles):
    for s in range(PEER_GROUP):
        for j in range(PEER_SLOTS):
            r = pl.multiple_of(idx_ref[slot, s, j], ROW_WORDS)
            tiles[s][pl.ds(j * ROW_WORDS, ROW_WORDS), :] = tab_ref[pl.ds(r, ROW_WORDS), :]


def _pipelined_groups(rows_hbm, idx_ref, sem, tab_ref, tiles, n_groups, compute):
    step, n_steps = pl.program_id(0), pl.num_programs(0)
    first = step * n_groups
    last_group = n_steps * n_groups - 1

    def idx_copy(g, slot):
        src = rows_hbm.at[jnp.minimum(first + g, last_group)]
        return pltpu.make_async_copy(src, idx_ref.at[slot], sem.at[slot])

    @pl.when(step == 0)
    def _():
        for g in range(IDX_SLOTS - 1):
            idx_copy(g, g).start()
        idx_copy(0, 0).wait()
        _gather_group(idx_ref, 0, tab_ref, tiles)
        idx_copy(IDX_SLOTS - 1, IDX_SLOTS - 1).start()

    def body(it, carry):
        for k in range(IDX_SLOTS):
            g = IDX_SLOTS * it + k
            nxt = (k + 1) % IDX_SLOTS
            idx_copy(g + 1, nxt).wait()
            compute(g)
            _gather_group(idx_ref, nxt, tab_ref, tiles)
            idx_copy(g + IDX_SLOTS, k).start()
        return carry

    lax.fori_loop(0, n_groups // IDX_SLOTS, body, 0)

    @pl.when(step == n_steps - 1)
    def _():
        for slot in range(1, IDX_SLOTS):
            idx_copy(n_groups, slot).wait()


def _group_rows(g):
    t0 = g * PEER_GROUP
    base = pl.multiple_of((t0 // PACK_ROWS) * PACK_ROWS, PACK_ROWS)
    return base, t0 - base


def _peer_u_kernel(rows_hbm, h2_ref, g_ref, tab_ref, coef_ref, idx_ref, sem, *scratch):
    tiles, acc_lo, acc_hi = scratch[:PEER_GROUP], scratch[PEER_GROUP], scratch[PEER_GROUP + 1]
    tt = h2_ref.shape[0]
    width = 2 * PEER_SLOTS
    sub = lax.broadcasted_iota(I32, (PACK_ROWS, width), 0)
    acc_lo[...] = jnp.zeros(acc_lo.shape, F32)
    acc_hi[...] = jnp.zeros(acc_hi.shape, F32)

    def compute(g):
        base, r0 = _group_rows(g)
        xg = h2_ref[pl.ds(base, PACK_ROWS), :]
        lhs = [jnp.concatenate([xg[:, c * LANES:(c + 1) * LANES],
                                xg[:, D_HALF + c * LANES:D_HALF + (c + 1) * LANES]], axis=0)
               for c in range(ROW_WORDS)]
        lo = acc_lo[pl.ds(base, PACK_ROWS), :]
        hi = acc_hi[pl.ds(base, PACK_ROWS), :]
        for s in range(PEER_GROUP):
            res = _dot_t(lhs[0], _chunk_operand(tiles[s], 0))
            for c in range(1, ROW_WORDS):
                res = res + _dot_t(lhs[c], _chunk_operand(tiles[s], c))
            lo = jnp.where(sub == r0 + s, res[:PACK_ROWS], lo)
            hi = jnp.where(sub == r0 + s, res[PACK_ROWS:], hi)
        acc_lo[pl.ds(base, PACK_ROWS), :] = lo
        acc_hi[pl.ds(base, PACK_ROWS), :] = hi

    _pipelined_groups(rows_hbm, idx_ref, sem, tab_ref, tiles, tt // PEER_GROUP, compute)
    col = lax.broadcasted_iota(I32, (tt, width), 1)
    halves = jnp.where(col % 2 == 0, acc_lo[...], acc_hi[...])
    act = _gelu(halves + pltpu.roll(halves, width - 1, 1))
    src = lax.broadcasted_iota(I32, (PEER_SLOTS, width), 0)
    dst = lax.broadcasted_iota(I32, (PEER_SLOTS, width), 1)
    spread = jnp.where(dst == 2 * src, 1.0, 0.0).astype(BF16)
    coef = act * _dot(g_ref[...].astype(BF16), spread)
    coef_ref[:, :width] = coef.astype(BF16)
    coef_ref[:, width:] = pltpu.roll(coef, 1, 1).astype(BF16)


def _gather_scratch():
    return ([pltpu.SMEM((IDX_SLOTS, PEER_GROUP, PEER_SLOTS), I32), pltpu.SemaphoreType.DMA((IDX_SLOTS,))]
            + [pltpu.VMEM((ROW_WORDS * PEER_SLOTS, LANES), I32) for _ in range(PEER_GROUP)])


def _peer_u_call(rows, h2, g, u_tab, tt):
    t = h2.shape[0]
    tok = lambda i: (i, 0)
    width = 2 * PEER_SLOTS
    return pl.pallas_call(
        _peer_u_kernel,
        grid=(t // tt,),
        in_specs=[
            pl.BlockSpec(memory_space=pl.ANY),
            pl.BlockSpec((tt, D_MODEL), tok),
            pl.BlockSpec((tt, PEER_SLOTS), tok),
            pl.BlockSpec(u_tab.shape, lambda i: (0, 0), pipeline_mode=pl.Buffered(1)),
        ],
        out_specs=pl.BlockSpec((tt, 2 * width), tok),
        out_shape=jax.ShapeDtypeStruct((t, 2 * width), BF16),
        scratch_shapes=_gather_scratch() + [pltpu.VMEM((tt, width), F32)] * 2,
        compiler_params=pltpu.CompilerParams(dimension_semantics=("arbitrary",),
                                             vmem_limit_bytes=VMEM_LIMIT_TABLE),
        name="peer_u",
    )(rows, h2, g, u_tab)


def _peer_v_kernel(rows_hbm, coef_ref, x1_ref, gf_ref, tab_ref, out_ref, idx_ref, sem, *scratch):
    tiles, acc_ref = scratch[:PEER_GROUP], scratch[PEER_GROUP]
    tt = coef_ref.shape[0]
    width = 2 * PEER_SLOTS
    sub = lax.broadcasted_iota(I32, (PACK_ROWS, LANES), 0)
    acc_ref[...] = jnp.zeros(acc_ref.shape, F32)

    def compute(g):
        base, r0 = _group_rows(g)
        cg = jnp.concatenate([coef_ref[pl.ds(base, PACK_ROWS), :width],
                              coef_ref[pl.ds(base, PACK_ROWS), width:]], axis=0)
        cols = [h * D_HALF + c * LANES for c in range(ROW_WORDS) for h in range(2)]
        outs = [acc_ref[pl.ds(base, PACK_ROWS), o:o + LANES] for o in cols]
        for s in range(PEER_GROUP):
            for c in range(ROW_WORDS):
                res = _dot(cg, _chunk_operand(tiles[s], c))
                outs[2 * c] = jnp.where(sub == r0 + s, res[:PACK_ROWS], outs[2 * c])
                outs[2 * c + 1] = jnp.where(sub == r0 + s, res[PACK_ROWS:], outs[2 * c + 1])
        for o, val in zip(cols, outs):
            acc_ref[pl.ds(base, PACK_ROWS), o:o + LANES] = val

    _pipelined_groups(rows_hbm, idx_ref, sem, tab_ref, tiles, tt // PEER_GROUP, compute)
    out_ref[...] = _rms(x1_ref[...] + acc_ref[...], gf_ref[...])


def _peer_v_call(rows, coef, x1, gf, v_tab, tt):
    t = x1.shape[0]
    tok = lambda i: (i, 0)
    return pl.pallas_call(
        _peer_v_kernel,
        grid=(t // tt,),
        in_specs=[
            pl.BlockSpec(memory_space=pl.ANY),
            pl.BlockSpec((tt, 4 * PEER_SLOTS), tok),
            pl.BlockSpec((tt, D_MODEL), tok),
            pl.BlockSpec((1, D_MODEL), lambda i: (0, 0)),
            pl.BlockSpec(v_tab.shape, lambda i: (0, 0), pipeline_mode=pl.Buffered(1)),
        ],
        out_specs=pl.BlockSpec((tt, D_MODEL), tok),
        out_shape=jax.ShapeDtypeStruct((t, D_MODEL), F32),
        scratch_shapes=_gather_scratch() + [pltpu.VMEM((tt, D_MODEL), F32)],
        compiler_params=pltpu.CompilerParams(dimension_semantics=("arbitrary",),
                                             vmem_limit_bytes=VMEM_LIMIT_TABLE),
        name="peer_v",
    )(rows, coef, x1, gf, v_tab)


def _pack_rows(tab):
    bits = lax.bitcast_convert_type(tab.astype(BF16), jnp.uint16).astype(jnp.uint32)
    word = bits[:, :D_HALF] | (bits[:, D_HALF:] << 16)
    return lax.bitcast_convert_type(word, I32).reshape(tab.shape[0] * ROW_WORDS, LANES)


def _rope_tables(seq):
    inv = 1.0 / (ROPE_THETA ** (jnp.arange(0, DA_HEAD_DIM, 2, dtype=F32) / DA_HEAD_DIM))
    ang = jnp.arange(seq, dtype=F32)[:, None] * inv[None, :]
    cos, sin = jnp.cos(ang), jnp.sin(ang)
    reps = LANES // DA_HEAD_DIM
    cos_t = jnp.tile(jnp.concatenate([cos, cos], axis=1), (1, reps))
    sin_t = jnp.tile(jnp.concatenate([-sin, sin], axis=1), (1, reps))
    return cos_t, sin_t


def kernel(x, norm1_g, w_in, lambda_q1, lambda_k1, lambda_q2, lambda_k2, subln_g, sgu_ln_g, sgu_ln_b, w_spatial, b_spatial, w_branch_a, w_branch_b, w_gate, b_gate, w_out, norm2_g, peer_wq, peer_subkeys, peer_u, peer_v, norm_f_g):
    b, s, d = x.shape
    t = b * s
    l = 0
    tq = min(ATTN_TQ, s)
    tt = min(PEER_TT, t)
    assert d == D_MODEL and w_in.shape[0] == 1 and peer_u.shape[1:] == (PEER_N_EXPERTS, D_MODEL)
    assert s % tq == 0 and tq % SG_BLOCK == 0 and t % min(MIX_TM, t) == 0 and t % min(TOPK_TT, t) == 0
    assert t % tt == 0 and (tt // PEER_GROUP) % IDX_SLOTS == 0

    x2 = x.reshape(t, d)
    cos_t, sin_t = _rope_tables(s)
    w_cat = jnp.concatenate([w_in[l], w_gate[l]], axis=1).astype(BF16)

    q, k0, k1, vt, u_act, sv_ln, gates = _proj_call(
        x2, norm1_g[l][None], w_cat, b_gate[l][None], cos_t, sin_t, sgu_ln_g[l][None], sgu_ln_b[l][None],
        seq=s, tm=tq)

    lam4 = jnp.stack([lambda_q1[l], lambda_k1[l], lambda_q2[l], lambda_k2[l]]).astype(F32)
    ya = _attn_call(lam4, q, k0, k1, vt, subln_g[l][None], batch=b, tq=tq)

    b_s_full = jnp.repeat(b_spatial[l].T, SG_GROUP_DIM, axis=1)
    sk = peer_subkeys[l].reshape(2 * PEER_HEADS, PEER_NKEYS, PEER_HALF).astype(BF16)
    x1, h2, sc = _mix_call(x2, u_act, sv_ln, gates, ya, w_spatial[l], b_s_full,
                           w_branch_a[l].astype(BF16), w_branch_b[l].astype(BF16), w_out[l].astype(BF16),
                           norm2_g[l][None], peer_wq[l].astype(BF16), sk, tm=min(MIX_TM, t))

    rows, g = _topk_call(sc, tt=min(TOPK_TT, t))
    rows = rows.reshape(t // PEER_GROUP, PEER_GROUP, PEER_SLOTS)
    coef = _peer_u_call(rows, h2, g, _pack_rows(peer_u[l]), tt=tt)
    out = _peer_v_call(rows, coef, x1, norm_f_g[None], _pack_rows(peer_v[l]), tt=tt)
    return out.reshape(b, s, d)
```

```python
import functools
import math

import jax
import jax.numpy as jnp
from jax import lax
from jax.experimental import pallas as pl
from jax.experimental.pallas import tpu as pltpu

F32 = jnp.float32
BF16 = jnp.bfloat16
I32 = jnp.int32

D_MODEL = 1024
CHUNK = 64
ROPE_THETA = 10000.0
EPS = 1e-6

DA_HEADS = 4
DA_HEAD_DIM = 64
DA_V_DIM = 2 * DA_HEAD_DIM
DA_WIDTH = DA_HEADS * DA_V_DIM

SG_GROUPS = 4
SG_BLOCK = 128
SG_GROUP_DIM = 128
SG_WIDTH = SG_GROUPS * SG_GROUP_DIM

PEER_HEADS = 8
PEER_NKEYS = 128
PEER_HALF = 128
PEER_TOPK = 16
PEER_SLOTS = PEER_HEADS * PEER_TOPK
PEER_N_EXPERTS = PEER_NKEYS * PEER_NKEYS

LANES = 128
SUBLANES = 8
PACK_ROWS = 16
PEER_GROUP = 8
D_HALF = D_MODEL // 2
ROW_WORDS = D_HALF // LANES
IDX_SLOTS = 4

ATTN_TQ = 1024
MIX_TM = 512
TOPK_TT = 128
PEER_TT = 512
PACK_TM = 512

VMEM_LIMIT = 48 * 1024 * 1024
VMEM_LIMIT_TABLE = 56 * 1024 * 1024

LAMBDA_INIT = 0.8 - 0.6 * math.exp(-0.3 * 0)
LOG2E = math.log2(math.e)
NEG_BIG = -1e30


def _gelu(x):
    return 0.5 * x * (1.0 + lax.erf(x * (0.5 ** 0.5)))


def _rms(x, g):
    return x * lax.rsqrt(jnp.mean(x * x, axis=-1, keepdims=True) + EPS) * g


def _dot(a, b):
    return jnp.dot(a, b, preferred_element_type=F32)


def _dot_t(a, b):
    return lax.dot_general(a, b, (((1,), (1,)), ((), ())), preferred_element_type=F32)


def _proj_kernel(x_ref, g1_ref, w_ref, bg_ref, cos_ref, sin_ref, lng_ref, lnb_ref,
                 qt_ref, k0_ref, k1_ref, vt_ref, u_ref, sv_ref, gate_ref):
    tm = x_ref.shape[0]
    hb = _rms(x_ref[...], g1_ref[...]).astype(BF16)
    cos = cos_ref[...]
    sin = sin_ref[...]
    lane = lax.broadcasted_iota(I32, (tm, LANES), 1)
    first_half = (lane % DA_HEAD_DIM) < (DA_HEAD_DIM // 2)

    def rope(t):
        outs = []
        for b in range(DA_WIDTH // LANES):
            blk = t[:, b * LANES:(b + 1) * LANES]
            swapped = jnp.where(first_half, pltpu.roll(blk, LANES - DA_HEAD_DIM // 2, 1),
                                pltpu.roll(blk, DA_HEAD_DIM // 2, 1))
            outs.append(blk * cos + swapped * sin)
        return jnp.concatenate(outs, axis=1)

    o = 0
    q = _dot(hb, w_ref[:, o:o + DA_WIDTH]); o += DA_WIDTH
    qt_ref[0] = (rope(q) * (DA_HEAD_DIM ** -0.5 * LOG2E)).T.astype(BF16)
    k = rope(_dot(hb, w_ref[:, o:o + DA_WIDTH])); o += DA_WIDTH
    comp0 = (lax.broadcasted_iota(I32, k.shape, 1) % DA_V_DIM) < DA_HEAD_DIM
    k0_ref[...] = jnp.where(comp0, k, 0.0).astype(BF16)
    k1_ref[...] = jnp.where(comp0, 0.0, k).astype(BF16)
    vt_ref[0] = _dot(hb, w_ref[:, o:o + DA_WIDTH]).T.astype(BF16); o += DA_WIDTH
    u_ref[...] = _gelu(_dot(hb, w_ref[:, o:o + SG_WIDTH])).astype(BF16); o += SG_WIDTH
    sv = _gelu(_dot(hb, w_ref[:, o:o + SG_WIDTH])); o += SG_WIDTH
    mu = jnp.mean(sv, axis=-1, keepdims=True)
    svc = sv - mu
    sv_ref[...] = (svc * lax.rsqrt(jnp.mean(svc * svc, axis=-1, keepdims=True) + EPS) * lng_ref[...]
                   + lnb_ref[...]).astype(BF16)
    gate_ref[...] = jax.nn.sigmoid(_dot(hb, w_ref[:, o:o + 2 * D_MODEL]) + bg_ref[...]).astype(BF16)


def _proj_call(x2, g1, w_cat, b_gate, cos_t, sin_t, ln_g, ln_b, seq, tm):
    t = x2.shape[0]
    nseq = seq // tm
    row = lambda i: (i, 0)
    fixed = lambda i: (0, 0)
    pos = lambda i: (i % nseq, 0)
    wcols = w_cat.shape[1]
    transposed = jax.ShapeDtypeStruct((t // tm, DA_WIDTH, tm), BF16)
    tblk = pl.BlockSpec((1, DA_WIDTH, tm), lambda i: (i, 0, 0))
    out_shapes = [transposed] + [jax.ShapeDtypeStruct((t, DA_WIDTH), BF16)] * 2 + [transposed] + \
                 [jax.ShapeDtypeStruct((t, SG_WIDTH), BF16)] * 2 + \
                 [jax.ShapeDtypeStruct((t, 2 * D_MODEL), BF16)]
    return pl.pallas_call(
        _proj_kernel,
        grid=(t // tm,),
        in_specs=[
            pl.BlockSpec((tm, D_MODEL), row),
            pl.BlockSpec((1, D_MODEL), fixed),
            pl.BlockSpec((D_MODEL, wcols), fixed, pipeline_mode=pl.Buffered(1)),
            pl.BlockSpec((1, 2 * D_MODEL), fixed),
            pl.BlockSpec((tm, LANES), pos),
            pl.BlockSpec((tm, LANES), pos),
            pl.BlockSpec((1, SG_WIDTH), fixed),
            pl.BlockSpec((1, SG_WIDTH), fixed),
        ],
        out_specs=[tblk] + [pl.BlockSpec((tm, DA_WIDTH), row)] * 2 + [tblk] +
                  [pl.BlockSpec((tm, SG_WIDTH), row)] * 2 +
                  [pl.BlockSpec((tm, 2 * D_MODEL), row)],
        out_shape=out_shapes,
        compiler_params=pltpu.CompilerParams(dimension_semantics=("arbitrary",), vmem_limit_bytes=VMEM_LIMIT),
        name="proj",
    )(x2, g1, w_cat, b_gate, cos_t, sin_t, ln_g, ln_b)


def _attn_kernel(lam_ref, qt_ref, k0_ref, k1_ref, vt_ref, sg_ref, o_ref, acc_ref, st_ref, m_ref, mprev_ref, l_ref,
                 *, tq):
    i = pl.program_id(2)
    qt = qt_ref[0]

    m_ref[...] = jnp.full(m_ref.shape, NEG_BIG, F32)
    l_ref[...] = jnp.zeros(l_ref.shape, F32)
    acc_ref[...] = jnp.zeros(acc_ref.shape, F32)

    def scores(j, masked):
        start = pl.multiple_of(j * tq, tq)
        if masked:
            keyc = lax.broadcasted_iota(I32, (tq, tq), 0) // CHUNK
            qryc = lax.broadcasted_iota(I32, (tq, tq), 1) // CHUNK
            visible = keyc <= qryc
        for c, k_ref in enumerate((k0_ref, k1_ref)):
            st = _dot(k_ref[pl.ds(start, tq), :], qt)
            if masked:
                st = jnp.where(visible, st, -jnp.inf)
            st_ref[c] = st
            m_old = m_ref[c]
            mprev_ref[c] = m_old
            m_ref[c] = jnp.maximum(m_old, jnp.max(st, axis=0, keepdims=True))

    def accumulate(j):
        vt = vt_ref[j]
        for c in range(2):
            m_new = m_ref[c]
            alpha = jnp.exp2(mprev_ref[c] - m_new)
            p = jnp.exp2(st_ref[c] - m_new)
            l_ref[c] = alpha * l_ref[c] + jnp.sum(p, axis=0, keepdims=True)
            acc_ref[c] = alpha * acc_ref[c] + _dot(vt, p.astype(BF16))

    scores(i, True)

    def body(j, prev):
        accumulate(prev)
        scores(j, False)
        return j

    accumulate(lax.fori_loop(0, i, body, i))

    lam = (jnp.exp(jnp.sum(lam_ref[0:1, :] * lam_ref[1:2, :], axis=-1, keepdims=True))
           - jnp.exp(jnp.sum(lam_ref[2:3, :] * lam_ref[3:4, :], axis=-1, keepdims=True)) + LAMBDA_INIT)
    o = acc_ref[0] / l_ref[0] - lam * (acc_ref[1] / l_ref[1])
    o = o * lax.rsqrt(jnp.mean(o * o, axis=0, keepdims=True) + EPS)
    o_ref[...] = (o.T * sg_ref[...] * (1.0 - LAMBDA_INIT)).astype(BF16)


def _attn_call(lam4, qt, k0, k1, vt, subln_g, batch, tq):
    t = k0.shape[0]
    s = t // batch
    nq = s // tq
    kern = functools.partial(_attn_kernel, tq=tq)
    kblk = pl.BlockSpec((s, DA_V_DIM), lambda bi, h, i: (bi, h))
    stat = pltpu.VMEM((2, 1, tq), F32)
    return pl.pallas_call(
        kern,
        grid=(batch, DA_HEADS, nq),
        in_specs=[
            pl.BlockSpec((4, DA_HEAD_DIM), lambda bi, h, i: (0, 0)),
            pl.BlockSpec((1, DA_V_DIM, tq), lambda bi, h, i: (bi * nq + i, h, 0)),
            kblk, kblk,
            pl.BlockSpec((nq, DA_V_DIM, tq), lambda bi, h, i: (bi, h, 0)),
            pl.BlockSpec((1, DA_V_DIM), lambda bi, h, i: (0, 0)),
        ],
        out_specs=pl.BlockSpec((tq, DA_V_DIM), lambda bi, h, i: (bi * nq + i, h)),
        out_shape=jax.ShapeDtypeStruct((t, DA_WIDTH), BF16),
        scratch_shapes=[
            pltpu.VMEM((2, DA_V_DIM, tq), F32),
            pltpu.VMEM((2, tq, tq), F32),
            stat, stat, stat,
        ],
        compiler_params=pltpu.CompilerParams(dimension_semantics=("arbitrary",) * 3, vmem_limit_bytes=VMEM_LIMIT),
        name="attn",
    )(lam4, qt, k0, k1, vt, subln_g)


def _mix_kernel(x_ref, u_ref, sv_ref, gate_ref, ya_ref, ws_ref, bs_ref, wa_ref, wb_ref, wo_ref, g2_ref,
                wq_ref, sk_ref, x1_ref, h2_ref, sc_ref):
    tm = x_ref.shape[0]
    pc = lax.broadcasted_iota(I32, (SG_BLOCK, SG_BLOCK), 0) // CHUNK
    pr = lax.broadcasted_iota(I32, (SG_BLOCK, SG_BLOCK), 1) // CHUNK
    causal = pr <= pc
    rows = []
    for n in range(tm // SG_BLOCK):
        cols = []
        for g in range(SG_GROUPS):
            w = jnp.where(causal, ws_ref[g], 0.0).astype(BF16)
            vb = sv_ref[n * SG_BLOCK:(n + 1) * SG_BLOCK, g * SG_GROUP_DIM:(g + 1) * SG_GROUP_DIM]
            cols.append(_dot(w, vb))
        rows.append(jnp.concatenate(cols, axis=1) + bs_ref[...])
    s = jnp.concatenate(rows, axis=0)
    yb = (u_ref[...].astype(F32) * s).astype(BF16)
    merged = (gate_ref[:, :D_MODEL].astype(F32) * _dot(ya_ref[...], wa_ref[...])
              + gate_ref[:, D_MODEL:].astype(F32) * _dot(yb, wb_ref[...]))
    x1 = x_ref[...] + _dot(merged.astype(BF16), wo_ref[...])
    x1_ref[...] = x1
    h2 = _rms(x1, g2_ref[...]).astype(BF16)
    h2_ref[...] = h2
    qp = _dot(h2, wq_ref[...]).astype(BF16)
    for hp in range(2 * PEER_HEADS):
        sc_ref[hp] = _dot_t(sk_ref[hp], qp[:, hp * PEER_HALF:(hp + 1) * PEER_HALF])


def _mix_call(x2, u_act, sv_ln, gates, ya, w_s, b_s_full, w_a, w_b, w_o, g2, w_q, sk, tm):
    t = x2.shape[0]
    row = lambda i: (i, 0)
    fixed2 = lambda i: (0, 0)
    fixed3 = lambda i: (0, 0, 0)
    return pl.pallas_call(
        _mix_kernel,
        grid=(t // tm,),
        in_specs=[
            pl.BlockSpec((tm, D_MODEL), row),
            pl.BlockSpec((tm, SG_WIDTH), row),
            pl.BlockSpec((tm, SG_WIDTH), row),
            pl.BlockSpec((tm, 2 * D_MODEL), row),
            pl.BlockSpec((tm, DA_WIDTH), row),
            pl.BlockSpec((SG_GROUPS, SG_BLOCK, SG_BLOCK), fixed3),
            pl.BlockSpec((SG_BLOCK, SG_WIDTH), fixed2),
            pl.BlockSpec((DA_WIDTH, D_MODEL), fixed2),
            pl.BlockSpec((SG_WIDTH, D_MODEL), fixed2),
            pl.BlockSpec((D_MODEL, D_MODEL), fixed2),
            pl.BlockSpec((1, D_MODEL), fixed2),
            pl.BlockSpec((D_MODEL, 2 * PEER_HEADS * PEER_HALF), fixed2),
            pl.BlockSpec((2 * PEER_HEADS, PEER_NKEYS, PEER_HALF), fixed3),
        ],
        out_specs=[
            pl.BlockSpec((tm, D_MODEL), row),
            pl.BlockSpec((tm, D_MODEL), row),
            pl.BlockSpec((2 * PEER_HEADS, PEER_NKEYS, tm), lambda i: (0, 0, i)),
        ],
        out_shape=[
            jax.ShapeDtypeStruct((t, D_MODEL), F32),
            jax.ShapeDtypeStruct((t, D_MODEL), BF16),
            jax.ShapeDtypeStruct((2 * PEER_HEADS, PEER_NKEYS, t), F32),
        ],
        compiler_params=pltpu.CompilerParams(dimension_semantics=("arbitrary",), vmem_limit_bytes=VMEM_LIMIT),
        name="mix",
    )(x2, u_act, sv_ln, gates, ya, w_s, b_s_full, w_a, w_b, w_o, g2, w_q, sk)


def _top16(vals, n):
    tt = vals.shape[1]
    pos = lax.broadcasted_iota(I32, (n, tt), 0)
    nblk = n // SUBLANES
    row = lax.broadcasted_iota(I32, (SUBLANES, tt), 0)
    pos_blocks = [row + b * SUBLANES for b in range(nblk)]
    cur = vals
    out_v, out_i = [], []
    for _ in range(PEER_TOPK):
        level = [(cur[b * SUBLANES:(b + 1) * SUBLANES], pos_blocks[b]) for b in range(nblk)]
        while len(level) > 1:
            nxt = []
            for a in range(0, len(level) - 1, 2):
                (va, ia), (vb, ib) = level[a], level[a + 1]
                take_b = vb > va
                nxt.append((jnp.where(take_b, vb, va), jnp.where(take_b, ib, ia)))
            if len(level) % 2:
                nxt.append(level[-1])
            level = nxt
        v8, i8 = level[0]
        m = jnp.max(v8, axis=0, keepdims=True)
        idx = jnp.min(jnp.where(v8 == m, i8, n), axis=0, keepdims=True)
        out_v.append(m)
        out_i.append(idx)
        cur = jnp.where(pos == idx, -jnp.inf, cur)
    return out_v, out_i


def _topk_kernel(sc_ref, row_ref, g_ref):
    tt = sc_ref.shape[2]
    k = PEER_TOPK
    rows_all, g_all = [], []
    for h in range(PEER_HEADS):
        s1, i1 = _top16(sc_ref[2 * h], PEER_NKEYS)
        s2, i2 = _top16(sc_ref[2 * h + 1], PEER_NKEYS)
        s2m = jnp.concatenate(s2, axis=0)
        i2m = jnp.concatenate(i2, axis=0)
        keep = [k // (a + 1) for a in range(k)]
        n_cand = -(-sum(keep) // SUBLANES) * SUBLANES
        pad = n_cand - sum(keep)
        cand = jnp.concatenate([s1[a] + s2m[:keep[a]] for a in range(k)]
                               + [jnp.full((pad, tt), -jnp.inf, F32)], axis=0)
        cidx = jnp.concatenate([i1[a] * PEER_NKEYS + i2m[:keep[a]] for a in range(k)]
                               + [jnp.zeros((pad, tt), I32)], axis=0)
        top_s, pos = _top16(cand, n_cand)
        cpos = lax.broadcasted_iota(I32, (n_cand, tt), 0)
        e = [jnp.sum(jnp.where(cpos == pos[r], cidx, 0), axis=0, keepdims=True) for r in range(k)]
        w = [jnp.exp(top_s[r] - top_s[0]) for r in range(k)]
        den = w[0]
        for r in range(1, k):
            den = den + w[r]
        rows_all.append(jnp.concatenate(e, axis=0) * ROW_WORDS)
        g_all.append(jnp.concatenate(w, axis=0) / den)
    row_ref[...] = jnp.concatenate(rows_all, axis=0).T
    g_ref[...] = jnp.concatenate(g_all, axis=0).T


def _topk_call(sc, tt):
    t = sc.shape[2]
    blk = pl.BlockSpec((tt, PEER_SLOTS), lambda i: (i, 0))
    return pl.pallas_call(
        _topk_kernel,
        grid=(t // tt,),
        in_specs=[pl.BlockSpec((2 * PEER_HEADS, PEER_NKEYS, tt), lambda i: (0, 0, i))],
        out_specs=[blk, blk],
        out_shape=[jax.ShapeDtypeStruct((t, PEER_SLOTS), I32),
                   jax.ShapeDtypeStruct((t, PEER_SLOTS), F32)],
        compiler_params=pltpu.CompilerParams(dimension_semantics=("arbitrary",), vmem_limit_bytes=VMEM_LIMIT),
        name="topk",
    )(sc)


def _chunk_operand(tile_ref, c):
    return pltpu.bitcast(tile_ref[pl.ds(c, PEER_SLOTS, stride=ROW_WORDS), :], BF16)


def _gather_group(idx_ref, slot, tab_ref, tiles):
    for s in range(PEER_GROUP):
        for j in range(PEER_SLOTS):
            r = pl.multiple_of(idx_ref[slot, s, j], ROW_WORDS)
            tiles[s][pl.ds(j * ROW_WORDS, ROW_WORDS), :] = tab_ref[pl.ds(r, ROW_WORDS), :]


def _pipelined_groups(rows_hbm, idx_ref, sem, tab_ref, tiles, n_groups, compute):
    step, n_steps = pl.program_id(0), pl.num_programs(0)
    first = step * n_groups
    last_group = n_steps * n_groups - 1

    def idx_copy(g, slot):
        src = rows_hbm.at[jnp.minimum(first + g, last_group)]
        return pltpu.make_async_copy(src, idx_ref.at[slot], sem.at[slot])

    @pl.when(step == 0)
    def _():
        for g in range(IDX_SLOTS - 1):
            idx_copy(g, g).start()
        idx_copy(0, 0).wait()
        _gather_group(idx_ref, 0, tab_ref, tiles)
        idx_copy(IDX_SLOTS - 1, IDX_SLOTS - 1).start()

    def body(it, carry):
        for k in range(IDX_SLOTS):
            g = IDX_SLOTS * it + k
            nxt = (k + 1) % IDX_SLOTS
            idx_copy(g + 1, nxt).wait()
            compute(g)
            _gather_group(idx_ref, nxt, tab_ref, tiles)
            idx_copy(g + IDX_SLOTS, k).start()
        return carry

    lax.fori_loop(0, n_groups // IDX_SLOTS, body, 0)

    @pl.when(step == n_steps - 1)
    def _():
        for slot in range(1, IDX_SLOTS):
            idx_copy(n_groups, slot).wait()


def _group_rows(g):
    t0 = g * PEER_GROUP
    base = pl.multiple_of((t0 // PACK_ROWS) * PACK_ROWS, PACK_ROWS)
    return base, t0 - base


def _peer_u_kernel(rows_hbm, h2_ref, g_ref, tab_ref, coef_ref, idx_ref, sem, *scratch):
    tiles, acc_lo, acc_hi = scratch[:PEER_GROUP], scratch[PEER_GROUP], scratch[PEER_GROUP + 1]
    tt = h2_ref.shape[0]
    width = 2 * PEER_SLOTS
    sub = lax.broadcasted_iota(I32, (PACK_ROWS, width), 0)
    acc_lo[...] = jnp.zeros(acc_lo.shape, F32)
    acc_hi[...] = jnp.zeros(acc_hi.shape, F32)

    def compute(g):
        base, r0 = _group_rows(g)
        xg = h2_ref[pl.ds(base, PACK_ROWS), :]
        lhs = [jnp.concatenate([xg[:, c * LANES:(c + 1) * LANES],
                                xg[:, D_HALF + c * LANES:D_HALF + (c + 1) * LANES]], axis=0)
               for c in range(ROW_WORDS)]
        lo = acc_lo[pl.ds(base, PACK_ROWS), :]
        hi = acc_hi[pl.ds(base, PACK_ROWS), :]
        for s in range(PEER_GROUP):
            res = _dot_t(lhs[0], _chunk_operand(tiles[s], 0))
            for c in range(1, ROW_WORDS):
                res = res + _dot_t(lhs[c], _chunk_operand(tiles[s], c))
            lo = jnp.where(sub == r0 + s, res[:PACK_ROWS], lo)
            hi = jnp.where(sub == r0 + s, res[PACK_ROWS:], hi)
        acc_lo[pl.ds(base, PACK_ROWS), :] = lo
        acc_hi[pl.ds(base, PACK_ROWS), :] = hi

    _pipelined_groups(rows_hbm, idx_ref, sem, tab_ref, tiles, tt // PEER_GROUP, compute)
    col = lax.broadcasted_iota(I32, (tt, width), 1)
    halves = jnp.where(col % 2 == 0, acc_lo[...], acc_hi[...])
    act = _gelu(halves + pltpu.roll(halves, width - 1, 1))
    src = lax.broadcasted_iota(I32, (PEER_SLOTS, width), 0)
    dst = lax.broadcasted_iota(I32, (PEER_SLOTS, width), 1)
    spread = jnp.where(dst == 2 * src, 1.0, 0.0).astype(BF16)
    coef = act * _dot(g_ref[...].astype(BF16), spread)
    coef_ref[:, :width] = coef.astype(BF16)
    coef_ref[:, width:] = pltpu.roll(coef, 1, 1).astype(BF16)


def _gather_scratch():
    return ([pltpu.SMEM((IDX_SLOTS, PEER_GROUP, PEER_SLOTS), I32), pltpu.SemaphoreType.DMA((IDX_SLOTS,))]
            + [pltpu.VMEM((ROW_WORDS * PEER_SLOTS, LANES), I32) for _ in range(PEER_GROUP)])


def _peer_u_call(rows, h2, g, u_tab, tt):
    t = h2.shape[0]
    tok = lambda i: (i, 0)
    width = 2 * PEER_SLOTS
    return pl.pallas_call(
        _peer_u_kernel,
        grid=(t // tt,),
        in_specs=[
            pl.BlockSpec(memory_space=pl.ANY),
            pl.BlockSpec((tt, D_MODEL), tok),
            pl.BlockSpec((tt, PEER_SLOTS), tok),
            pl.BlockSpec(u_tab.shape, lambda i: (0, 0), pipeline_mode=pl.Buffered(1)),
        ],
        out_specs=pl.BlockSpec((tt, 2 * width), tok),
        out_shape=jax.ShapeDtypeStruct((t, 2 * width), BF16),
        scratch_shapes=_gather_scratch() + [pltpu.VMEM((tt, width), F32)] * 2,
        compiler_params=pltpu.CompilerParams(dimension_semantics=("arbitrary",),
                                             vmem_limit_bytes=VMEM_LIMIT_TABLE),
        name="peer_u",
    )(rows, h2, g, u_tab)


def _peer_v_kernel(rows_hbm, coef_ref, x1_ref, gf_ref, tab_ref, out_ref, idx_ref, sem, *scratch):
    tiles, acc_ref = scratch[:PEER_GROUP], scratch[PEER_GROUP]
    tt = coef_ref.shape[0]
    width = 2 * PEER_SLOTS
    sub = lax.broadcasted_iota(I32, (PACK_ROWS, LANES), 0)
    acc_ref[...] = jnp.zeros(acc_ref.shape, F32)

    def compute(g):
        base, r0 = _group_rows(g)
        cg = jnp.concatenate([coef_ref[pl.ds(base, PACK_ROWS), :width],
                              coef_ref[pl.ds(base, PACK_ROWS), width:]], axis=0)
        cols = [h * D_HALF + c * LANES for c in range(ROW_WORDS) for h in range(2)]
        outs = [acc_ref[pl.ds(base, PACK_ROWS), o:o + LANES] for o in cols]
        for s in range(PEER_GROUP):
            for c in range(ROW_WORDS):
                res = _dot(cg, _chunk_operand(tiles[s], c))
                outs[2 * c] = jnp.where(sub == r0 + s, res[:PACK_ROWS], outs[2 * c])
                outs[2 * c + 1] = jnp.where(sub == r0 + s, res[PACK_ROWS:], outs[2 * c + 1])
        for o, val in zip(cols, outs):
            acc_ref[pl.ds(base, PACK_ROWS), o:o + LANES] = val

    _pipelined_groups(rows_hbm, idx_ref, sem, tab_ref, tiles, tt // PEER_GROUP, compute)
    out_ref[...] = _rms(x1_ref[...] + acc_ref[...], gf_ref[...])


def _peer_v_call(rows, coef, x1, gf, v_tab, tt):
    t = x1.shape[0]
    tok = lambda i: (i, 0)
    return pl.pallas_call(
        _peer_v_kernel,
        grid=(t // tt,),
        in_specs=[
            pl.BlockSpec(memory_space=pl.ANY),
            pl.BlockSpec((tt, 4 * PEER_SLOTS), tok),
            pl.BlockSpec((tt, D_MODEL), tok),
            pl.BlockSpec((1, D_MODEL), lambda i: (0, 0)),
            pl.BlockSpec(v_tab.shape, lambda i: (0, 0), pipeline_mode=pl.Buffered(1)),
        ],
        out_specs=pl.BlockSpec((tt, D_MODEL), tok),
        out_shape=jax.ShapeDtypeStruct((t, D_MODEL), F32),
        scratch_shapes=_gather_scratch() + [pltpu.VMEM((tt, D_MODEL), F32)],
        compiler_params=pltpu.CompilerParams(dimension_semantics=("arbitrary",),
                                             vmem_limit_bytes=VMEM_LIMIT_TABLE),
        name="peer_v",
    )(rows, coef, x1, gf, v_tab)


def _pack_kernel(tab_ref, out_ref):
    rows = tab_ref.shape[0]
    x = tab_ref[...]
    lo = pltpu.bitcast(x[:, :D_HALF].astype(BF16).astype(F32), I32)
    hi = pltpu.bitcast(x[:, D_HALF:].astype(BF16).astype(F32), I32)
    word = lax.shift_right_logical(lo, 16) | (hi & jnp.int32(-(1 << 16)))
    for c in range(ROW_WORDS):
        out_ref[pl.ds(c, rows, stride=ROW_WORDS), :] = word[:, c * LANES:(c + 1) * LANES]


def _pack_rows(tab):
    n = tab.shape[0]
    rows = min(PACK_TM, n)
    return pl.pallas_call(
        _pack_kernel,
        grid=(n // rows,),
        in_specs=[pl.BlockSpec((rows, D_MODEL), lambda i: (i, 0))],
        out_specs=pl.BlockSpec((rows * ROW_WORDS, LANES), lambda i: (i, 0)),
        out_shape=jax.ShapeDtypeStruct((n * ROW_WORDS, LANES), I32),
        compiler_params=pltpu.CompilerParams(dimension_semantics=("arbitrary",), vmem_limit_bytes=VMEM_LIMIT),
        name="pack",
    )(tab)


def _rope_tables(seq):
    inv = 1.0 / (ROPE_THETA ** (jnp.arange(0, DA_HEAD_DIM, 2, dtype=F32) / DA_HEAD_DIM))
    ang = jnp.arange(seq, dtype=F32)[:, None] * inv[None, :]
    cos, sin = jnp.cos(ang), jnp.sin(ang)
    reps = LANES // DA_HEAD_DIM
    cos_t = jnp.tile(jnp.concatenate([cos, cos], axis=1), (1, reps))
    sin_t = jnp.tile(jnp.concatenate([-sin, sin], axis=1), (1, reps))
    return cos_t, sin_t


def kernel(x, norm1_g, w_in, lambda_q1, lambda_k1, lambda_q2, lambda_k2, subln_g, sgu_ln_g, sgu_ln_b, w_spatial, b_spatial, w_branch_a, w_branch_b, w_gate, b_gate, w_out, norm2_g, peer_wq, peer_subkeys, peer_u, peer_v, norm_f_g):
    b, s, d = x.shape
    t = b * s
    l = 0
    tq = min(ATTN_TQ, s)
    tt = min(PEER_TT, t)
    assert d == D_MODEL and w_in.shape[0] == 1 and peer_u.shape[1:] == (PEER_N_EXPERTS, D_MODEL)
    assert s % tq == 0 and tq % SG_BLOCK == 0 and t % min(MIX_TM, t) == 0 and t % min(TOPK_TT, t) == 0
    assert t % tt == 0 and (tt // PEER_GROUP) % IDX_SLOTS == 0

    x2 = x.reshape(t, d)
    cos_t, sin_t = _rope_tables(s)
    w_cat = jnp.concatenate([w_in[l], w_gate[l]], axis=1).astype(BF16)

    q, k0, k1, vt, u_act, sv_ln, gates = _proj_call(
        x2, norm1_g[l][None], w_cat, b_gate[l][None], cos_t, sin_t, sgu_ln_g[l][None], sgu_ln_b[l][None],
        seq=s, tm=tq)

    lam4 = jnp.stack([lambda_q1[l], lambda_k1[l], lambda_q2[l], lambda_k2[l]]).astype(F32)
    ya = _attn_call(lam4, q, k0, k1, vt, subln_g[l][None], batch=b, tq=tq)

    b_s_full = jnp.repeat(b_spatial[l].T, SG_GROUP_DIM, axis=1)
    sk = peer_subkeys[l].reshape(2 * PEER_HEADS, PEER_NKEYS, PEER_HALF).astype(BF16)
    x1, h2, sc = _mix_call(x2, u_act, sv_ln, gates, ya, w_spatial[l], b_s_full,
                           w_branch_a[l].astype(BF16), w_branch_b[l].astype(BF16), w_out[l].astype(BF16),
                           norm2_g[l][None], peer_wq[l].astype(BF16), sk, tm=min(MIX_TM, t))

    rows, g = _topk_call(sc, tt=min(TOPK_TT, t))
    rows = rows.reshape(t // PEER_GROUP, PEER_GROUP, PEER_SLOTS)
    coef = _peer_u_call(rows, h2, g, _pack_rows(peer_u[l]), tt=tt)
    out = _peer_v_call(rows, coef, x1, norm_f_g[None], _pack_rows(peer_v[l]), tt=tt)
    return out.reshape(b, s, d)
```
